```python
import math
import jax, jax.numpy as jnp
from jax import lax
import numpy as np

D_MODEL = 2048
BATCH = 1
SEQ = 8192
DEPTH = 2

N_HEADS = 8
HEAD_DIM = 128
ATTN_W = N_HEADS * HEAD_DIM
CONV_CH = D_MODEL // 2
CONV_K = 31
FFN_HIDDEN = ((8 * D_MODEL // 3 + 255) // 256) * 256
PLE_DIM = 256
Q_BLOCK = 128
EPS = 1e-6
IN_COLS = 2 * CONV_CH + 3 * ATTN_W + N_HEADS + 2 * D_MODEL
NEG_INF = -1e30

kernel_name = "hybrid_conformer_fox_gated_trunk"


def rmsnorm(x, g):
    xf = x.astype(jnp.float32)
    y = xf * lax.rsqrt(jnp.mean(xf * xf, axis=-1, keepdims=True) + EPS)
    return (y * g.astype(jnp.float32)).astype(x.dtype)


def layernorm(x, g, b):
    xf = x.astype(jnp.float32)
    mu = jnp.mean(xf, axis=-1, keepdims=True)
    var = jnp.mean(jnp.square(xf - mu), axis=-1, keepdims=True)
    y = (xf - mu) * lax.rsqrt(var + EPS)
    return (y * g.astype(jnp.float32) + b.astype(jnp.float32)).astype(x.dtype)


def conformer_conv(glu_in, conv_w, conv_b, ln_g, ln_b, w_conv_out):
    a, gate = jnp.split(glu_in, 2, axis=-1)
    u = a * jax.nn.sigmoid(gate)
    u = lax.conv_general_dilated(
        u, conv_w[:, None, :].astype(u.dtype),
        window_strides=(1,), padding=[(CONV_K - 1, 0)],
        dimension_numbers=("NWC", "WIO", "NWC"),
        feature_group_count=CONV_CH) + conv_b
    u = jax.nn.silu(layernorm(u, ln_g, ln_b))
    return u @ w_conv_out


def forgetting_attention(q, k, v, log_f):
    b, s, h, d = q.shape
    nb = s // Q_BLOCK
    scale = 1.0 / math.sqrt(d)
    c = jnp.cumsum(log_f, axis=1).transpose(0, 2, 1)
    q_blocks = q.reshape(b, nb, Q_BLOCK, h, d).transpose(1, 0, 2, 3, 4)
    c_blocks = c.reshape(b, h, nb, Q_BLOCK).transpose(2, 0, 1, 3)
    starts = jnp.arange(nb, dtype=jnp.int32) * Q_BLOCK
    k_pos = jnp.arange(s, dtype=jnp.int32)

    def one_block(args):
        qb, cb, start = args
        sc = jnp.einsum("bqhd,bkhd->bhqk", qb, k,
                        preferred_element_type=jnp.float32) * scale
        sc = sc + cb[..., :, None] - c[:, :, None, :]
        q_pos = start + jnp.arange(Q_BLOCK, dtype=jnp.int32)
        mask = k_pos[None, :] <= q_pos[:, None]
        sc = jnp.where(mask, sc, NEG_INF)
        pr = jax.nn.softmax(sc, axis=-1)
        return jnp.einsum("bhqk,bkhd->bqhd", pr.astype(v.dtype), v)

    out = lax.map(one_block, (q_blocks, c_blocks, starts))
    return out.transpose(1, 0, 2, 3, 4).reshape(b, s, h * d)


def setup_inputs(seed: int = 0) -> dict:
    key = jax.random.key(seed)
    ks = jax.random.split(key, 20)
    f32 = jnp.float32

    def w(k, shape, fan_in):
        return jax.random.normal(k, shape, f32) * (fan_in ** -0.5)

    def gain(k, shape):
        return 1.0 + 0.02 * jax.random.normal(k, shape, f32)

    return {
        "x": jax.random.normal(ks[0], (BATCH, SEQ, D_MODEL), f32),
        "p": jax.random.normal(ks[1], (DEPTH, BATCH, SEQ, PLE_DIM), f32),
        "norm_mix_g": gain(ks[2], (DEPTH, D_MODEL)),
        "w_in": w(ks[3], (DEPTH, D_MODEL, IN_COLS), D_MODEL),
        "b_forget": 2.0 + 0.1 * jax.random.normal(ks[4], (DEPTH, N_HEADS), f32),
        "conv_w": w(ks[5], (DEPTH, CONV_K, CONV_CH), CONV_K),
        "conv_b": 0.02 * jax.random.normal(ks[6], (DEPTH, CONV_CH), f32),
        "conv_ln_g": gain(ks[7], (DEPTH, CONV_CH)),
        "conv_ln_b": 0.02 * jax.random.normal(ks[8], (DEPTH, CONV_CH), f32),
        "w_conv_out": w(ks[9], (DEPTH, CONV_CH, D_MODEL), CONV_CH),
        "w_attn_out": w(ks[10], (DEPTH, ATTN_W, D_MODEL), ATTN_W),
        "w_out": w(ks[11], (DEPTH, D_MODEL, D_MODEL), D_MODEL),
        "norm_ffn_g": gain(ks[12], (DEPTH, D_MODEL)),
        "w_gate_up": w(ks[13], (DEPTH, D_MODEL, 2 * FFN_HIDDEN), D_MODEL),
        "w_down": w(ks[14], (DEPTH, FFN_HIDDEN, D_MODEL), FFN_HIDDEN),
        "norm_ple_g": gain(ks[15], (DEPTH, D_MODEL)),
        "w_ple_gate": w(ks[16], (DEPTH, D_MODEL, D_MODEL), D_MODEL),
        "w_ple_proj": w(ks[17], (DEPTH, PLE_DIM, D_MODEL), PLE_DIM),
        "final_g": gain(ks[18], (D_MODEL,)),
    }


def reference(x, p, norm_mix_g, w_in, b_forget, conv_w, conv_b, conv_ln_g, conv_ln_b,
              w_conv_out, w_attn_out, w_out, norm_ffn_g, w_gate_up, w_down,
              norm_ple_g, w_ple_gate, w_ple_proj, final_g):
    b, s, _ = x.shape
    split_pts = np.cumsum([2 * CONV_CH, ATTN_W, ATTN_W, ATTN_W, N_HEADS, D_MODEL]).tolist()
    for i in range(DEPTH):
        h = rmsnorm(x, norm_mix_g[i])
        proj = h @ w_in[i]
        glu_in, q, k, v, f_logit, g_conv, g_attn = jnp.split(proj, split_pts, axis=-1)

        y_conv = conformer_conv(glu_in, conv_w[i], conv_b[i], conv_ln_g[i],
                                conv_ln_b[i], w_conv_out[i])

        q = q.reshape(b, s, N_HEADS, HEAD_DIM)
        k = k.reshape(b, s, N_HEADS, HEAD_DIM)
        v = v.reshape(b, s, N_HEADS, HEAD_DIM)
        log_f = jax.nn.log_sigmoid((f_logit + b_forget[i]).astype(jnp.float32))
        y_attn = forgetting_attention(q, k, v, log_f) @ w_attn_out[i]

        merged = jax.nn.sigmoid(g_conv) * y_conv + jax.nn.sigmoid(g_attn) * y_attn
        x = x + merged @ w_out[i]

        hf = rmsnorm(x, norm_ffn_g[i])
        gate, up = jnp.split(hf @ w_gate_up[i], 2, axis=-1)
        x = x + (jax.nn.silu(gate) * up) @ w_down[i]

        hp = rmsnorm(x, norm_ple_g[i])
        x = x + jax.nn.sigmoid(hp @ w_ple_gate[i]) * (p[i] @ w_ple_proj[i])
    return rmsnorm(x, final_g)
```

```python
import functools
import math

import jax
import jax.numpy as jnp
from jax import lax
from jax.experimental import pallas as pl
from jax.experimental.pallas import tpu as pltpu

D_MODEL = 2048
N_HEADS = 8
HEAD_DIM = 128
ATTN_W = N_HEADS * HEAD_DIM
CONV_CH = D_MODEL // 2
CONV_K = 31
FFN_HIDDEN = 5632
EPS = 1e-6
NEG_INF = -1e30

LANES = 128
CONV_HALO = 32
VMEM_LIMIT_BYTES = 56 * 1024 * 1024

BF16 = jnp.bfloat16
F32 = jnp.float32


def _params(*sem):
    return pltpu.CompilerParams(dimension_semantics=sem, vmem_limit_bytes=VMEM_LIMIT_BYTES)


def _rms(x, g):
    ms = jnp.mean(x * x, axis=-1, keepdims=True)
    return x * lax.rsqrt(ms + EPS) * g


def _nmm_kernel(*refs, n_w, n_extra, epilogue):
    x_ref, g_ref = refs[0], refs[1]
    w_refs = refs[2:2 + n_w]
    e_refs = refs[2 + n_w:2 + n_w + n_extra]
    o_ref = refs[2 + n_w + n_extra]
    h_ref = refs[3 + n_w + n_extra]

    @pl.when(pl.program_id(1) == 0)
    def _():
        h_ref[...] = _rms(x_ref[...], g_ref[...]).astype(BF16)

    h = h_ref[...]
    accs = [jnp.dot(h, w[...], preferred_element_type=F32) for w in w_refs]
    o_ref[...] = epilogue(accs, [e[...] for e in e_refs]).astype(o_ref.dtype)


def _norm_matmul(x, g, ws, extras, epilogue, out_dtype, *, tm, tn, name):
    m, k = x.shape
    n = ws[0].shape[1]
    grid = (m // tm, n // tn)
    in_specs = [pl.BlockSpec((tm, k), lambda i, j: (i, 0)),
                pl.BlockSpec((1, k), lambda i, j: (0, 0))]
    in_specs += [pl.BlockSpec((k, tn), lambda i, j: (0, j)) for _ in ws]
    in_specs += [pl.BlockSpec((1, tn), lambda i, j: (0, j)) for _ in extras]
    return pl.pallas_call(
        functools.partial(_nmm_kernel, n_w=len(ws), n_extra=len(extras), epilogue=epilogue),
        grid=grid,
        in_specs=in_specs,
        out_specs=pl.BlockSpec((tm, tn), lambda i, j: (i, j)),
        out_shape=jax.ShapeDtypeStruct((m, n), out_dtype),
        scratch_shapes=[pltpu.VMEM((tm, k), BF16)],
        compiler_params=_params("parallel", "arbitrary"),
        name=name,
    )(x, g, *ws, *extras)


def _ep_identity(accs, extras):
    return accs[0]


def _ep_sigmoid(accs, extras):
    return jax.nn.sigmoid(accs[0])


def _ep_glu(accs, extras):
    return accs[0] * jax.nn.sigmoid(accs[1])


def _ep_log_forget(accs, extras):
    return jax.nn.log_sigmoid(accs[0] + extras[0])


def _cumsum_kernel(x_ref, o_ref, carry_ref, *, tb):
    @pl.when(pl.program_id(0) == 0)
    def _():
        carry_ref[...] = jnp.zeros_like(carry_ref)

    x = x_ref[...]
    row = lax.broadcasted_iota(jnp.int32, (tb, tb), 0)
    col = lax.broadcasted_iota(jnp.int32, (tb, tb), 1)
    tri = jnp.where(col <= row, 1.0, 0.0).astype(BF16)
    hi = x.astype(BF16)
    r1 = x - hi.astype(F32)
    mid = r1.astype(BF16)
    lo = (r1 - mid.astype(F32)).astype(BF16)
    s = (jnp.dot(tri, hi, preferred_element_type=F32)
         + jnp.dot(tri, mid, preferred_element_type=F32)
         + jnp.dot(tri, lo, preferred_element_type=F32))
    s = s + carry_ref[...]
    o_ref[...] = s
    carry_ref[...] = s[tb - 1:tb, :]


def _cumsum_rows(x, *, tb=256):
    m, n = x.shape
    return pl.pallas_call(
        functools.partial(_cumsum_kernel, tb=tb),
        grid=(m // tb,),
        in_specs=[pl.BlockSpec((tb, n), lambda i: (i, 0))],
        out_specs=pl.BlockSpec((tb, n), lambda i: (i, 0)),
        out_shape=jax.ShapeDtypeStruct((m, n), F32),
        scratch_shapes=[pltpu.VMEM((1, n), F32)],
        compiler_params=_params("arbitrary"),
        name="forget_cumsum",
    )(x)


def _conv_kernel(u_ref, halo_ref, w_ref, b_ref, lg_ref, lb_ref, o_ref, buf_ref, *, ts, rc):
    i = pl.program_id(0)
    halo = halo_ref[...]
    buf_ref[0:CONV_HALO, :] = jnp.where(i == 0, jnp.zeros_like(halo), halo)
    buf_ref[CONV_HALO:, :] = u_ref[...]
    off = CONV_HALO - (CONV_K - 1)
    for r in range(ts // rc):
        base = r * rc
        acc = jnp.broadcast_to(b_ref[...], (rc, CONV_CH))
        for k in range(CONV_K):
            acc = acc + buf_ref[base + off + k:base + off + k + rc, :] * w_ref[k:k + 1, :]
        mu = jnp.mean(acc, axis=-1, keepdims=True)
        d = acc - mu
        var = jnp.mean(d * d, axis=-1, keepdims=True)
        y = d * lax.rsqrt(var + EPS) * lg_ref[...] + lb_ref[...]
        o_ref[base:base + rc, :] = (y * jax.nn.sigmoid(y)).astype(o_ref.dtype)


def _conformer_conv(u, conv_w, conv_b, ln_g, ln_b, *, ts=256, rc=16):
    s, c = u.shape
    halo_blocks = ts // CONV_HALO
    return pl.pallas_call(
        functools.partial(_conv_kernel, ts=ts, rc=rc),
        grid=(s // ts,),
        in_specs=[pl.BlockSpec((ts, c), lambda i: (i, 0)),
                  pl.BlockSpec((CONV_HALO, c), lambda i: (jnp.maximum(i * halo_blocks - 1, 0), 0)),
                  pl.BlockSpec((CONV_K, c), lambda i: (0, 0)),
                  pl.BlockSpec((1, c), lambda i: (0, 0)),
                  pl.BlockSpec((1, c), lambda i: (0, 0)),
                  pl.BlockSpec((1, c), lambda i: (0, 0))],
        out_specs=pl.BlockSpec((ts, c), lambda i: (i, 0)),
        out_shape=jax.ShapeDtypeStruct((s, c), BF16),
        scratch_shapes=[pltpu.VMEM((ts + CONV_HALO, c), F32)],
        compiler_params=_params("parallel"),
        name="conformer_conv",
    )(u, u, conv_w, conv_b, ln_g, ln_b)


def _attn_kernel(q_ref, k_ref, v_ref, cq_ref, ck_ref, o_ref, m_ref, l_ref, acc_ref, *, scale, tq):
    qi = pl.program_id(1)
    kj = pl.program_id(2)

    @pl.when(kj == 0)
    def _():
        m_ref[...] = jnp.full_like(m_ref, NEG_INF)
        l_ref[...] = jnp.zeros_like(l_ref)
        acc_ref[...] = jnp.zeros_like(acc_ref)

    def step(diagonal):
        s = lax.dot_general(q_ref[...], k_ref[...], (((1,), (1,)), ((), ())),
                            preferred_element_type=F32) * scale
        s = s + cq_ref[...] - ck_ref[...]
        if diagonal:
            row = lax.broadcasted_iota(jnp.int32, (tq, tq), 0)
            col = lax.broadcasted_iota(jnp.int32, (tq, tq), 1)
            s = jnp.where(col <= row, s, NEG_INF)
        m_prev = m_ref[...]
        m_new = jnp.maximum(m_prev, jnp.max(s, axis=-1, keepdims=True))
        alpha = jnp.exp(m_prev - m_new)
        p = jnp.exp(s - m_new)
        l_ref[...] = alpha * l_ref[...] + jnp.sum(p, axis=-1, keepdims=True)
        acc_ref[...] = alpha * acc_ref[...] + jnp.dot(p.astype(BF16), v_ref[...],
                                                      preferred_element_type=F32)
        m_ref[...] = m_new

    @pl.when(kj < qi)
    def _():
        step(False)

    @pl.when(kj == qi)
    def _():
        step(True)
        o_ref[...] = (acc_ref[...] / l_ref[...]).astype(o_ref.dtype)


def _forgetting_attention(qkv, c_col, c_row, *, tq=512):
    s = qkv.shape[0]
    nq = s // tq
    scale = 1.0 / math.sqrt(HEAD_DIM)
    kv = lambda h, qi, kj: jnp.minimum(kj, qi)
    return pl.pallas_call(
        functools.partial(_attn_kernel, scale=scale, tq=tq),
        grid=(N_HEADS, nq, nq),
        in_specs=[pl.BlockSpec((tq, HEAD_DIM), lambda h, qi, kj: (qi, h)),
                  pl.BlockSpec((tq, HEAD_DIM), lambda h, qi, kj: (kv(h, qi, kj), N_HEADS + h)),
                  pl.BlockSpec((tq, HEAD_DIM), lambda h, qi, kj: (kv(h, qi, kj), 2 * N_HEADS + h)),
                  pl.BlockSpec((None, tq, 1), lambda h, qi, kj: (h, qi, 0)),
                  pl.BlockSpec((None, 1, tq), lambda h, qi, kj: (h, 0, kv(h, qi, kj)))],
        out_specs=pl.BlockSpec((tq, HEAD_DIM), lambda h, qi, kj: (qi, h)),
        out_shape=jax.ShapeDtypeStruct((s, ATTN_W), BF16),
        scratch_shapes=[pltpu.VMEM((tq, 1), F32), pltpu.VMEM((tq, 1), F32),
                        pltpu.VMEM((tq, HEAD_DIM), F32)],
        compiler_params=_params("parallel", "parallel", "arbitrary"),
        name="forgetting_attention",
    )(qkv, qkv, qkv, c_col, c_row)


def _merge_kernel(x_ref, ac_ref, at_ref, gc_ref, ga_ref, wc_ref, wa_ref, wo_ref, o_ref):
    yc = jnp.dot(ac_ref[...], wc_ref[...], preferred_element_type=F32)
    ya = jnp.dot(at_ref[...], wa_ref[...], preferred_element_type=F32)
    merged = gc_ref[...].astype(F32) * yc + ga_ref[...].astype(F32) * ya
    o_ref[...] = x_ref[...] + jnp.dot(merged.astype(BF16), wo_ref[...], preferred_element_type=F32)


def _merge(x, a_conv, a_attn, gates, w_conv_out, w_attn_out, w_out, *, tm=256):
    s, d = x.shape
    const = lambda i: (0, 0)
    return pl.pallas_call(
        _merge_kernel,
        grid=(s // tm,),
        in_specs=[pl.BlockSpec((tm, d), lambda i: (i, 0)),
                  pl.BlockSpec((tm, CONV_CH), lambda i: (i, 0)),
                  pl.BlockSpec((tm, ATTN_W), lambda i: (i, 0)),
                  pl.BlockSpec((tm, d), lambda i: (i, 0)),
                  pl.BlockSpec((tm, d), lambda i: (i, 1)),
                  pl.BlockSpec((CONV_CH, d), const),
                  pl.BlockSpec((ATTN_W, d), const),
                  pl.BlockSpec((d, d), const)],
        out_specs=pl.BlockSpec((tm, d), lambda i: (i, 0)),
        out_shape=jax.ShapeDtypeStruct((s, d), F32),
        compiler_params=_params("parallel"),
        name="gated_merge",
    )(x, a_conv, a_attn, gates, gates, w_conv_out, w_attn_out, w_out)


def _ffn_kernel(x_ref, g_ref, wg_ref, wu_ref, wd_ref, o_ref, h_ref):
    @pl.when(pl.program_id(1) == 0)
    def _():
        x = x_ref[...]
        h_ref[...] = _rms(x, g_ref[...]).astype(BF16)
        o_ref[...] = x

    h = h_ref[...]
    gate = jnp.dot(h, wg_ref[...], preferred_element_type=F32)
    up = jnp.dot(h, wu_ref[...], preferred_element_type=F32)
    a = (gate * jax.nn.sigmoid(gate) * up).astype(BF16)
    o_ref[...] += jnp.dot(a, wd_ref[...], preferred_element_type=F32)


def _ffn(x, g, w_gate_up, w_down, *, tm=512, th=512):
    s, d = x.shape
    nh = FFN_HIDDEN // th
    return pl.pallas_call(
        _ffn_kernel,
        grid=(s // tm, nh),
        in_specs=[pl.BlockSpec((tm, d), lambda i, c: (i, 0)),
                  pl.BlockSpec((1, d), lambda i, c: (0, 0)),
                  pl.BlockSpec((d, th), lambda i, c: (0, c)),
                  pl.BlockSpec((d, th), lambda i, c: (0, c + nh)),
                  pl.BlockSpec((th, d), lambda i, c: (c, 0))],
        out_specs=pl.BlockSpec((tm, d), lambda i, c: (i, 0)),
        out_shape=jax.ShapeDtypeStruct((s, d), F32),
        scratch_shapes=[pltpu.VMEM((tm, d), BF16)],
        compiler_params=_params("parallel", "arbitrary"),
        name="swiglu_ffn",
    )(x, g, w_gate_up, w_gate_up, w_down)


def _ple_kernel(x_ref, g_ref, p_ref, wg_ref, wp_ref, fg_ref, o_ref, *, final):
    x = x_ref[...]
    h = _rms(x, g_ref[...]).astype(BF16)
    gate = jax.nn.sigmoid(jnp.dot(h, wg_ref[...], preferred_element_type=F32))
    emb = jnp.dot(p_ref[...].astype(BF16), wp_ref[...], preferred_element_type=F32)
    y = x + gate * emb
    if final:
        y = _rms(y, fg_ref[...])
    o_ref[...] = y


def _ple(x, g, p, w_gate, w_proj, final_g, *, final, tm=256):
    s, d = x.shape
    pd = p.shape[1]
    const = lambda i: (0, 0)
    return pl.pallas_call(
        functools.partial(_ple_kernel, final=final),
        grid=(s // tm,),
        in_specs=[pl.BlockSpec((tm, d), lambda i: (i, 0)),
                  pl.BlockSpec((1, d), const),
                  pl.BlockSpec((tm, pd), lambda i: (i, 0)),
                  pl.BlockSpec((d, d), const),
                  pl.BlockSpec((pd, d), const),
                  pl.BlockSpec((1, d), const)],
        out_specs=pl.BlockSpec((tm, d), lambda i: (i, 0)),
        out_shape=jax.ShapeDtypeStruct((s, d), F32),
        compiler_params=_params("parallel"),
        name="ple_final" if final else "ple",
    )(x, g, p, w_gate, w_proj, final_g)


def kernel(x, p, norm_mix_g, w_in, b_forget, conv_w, conv_b, conv_ln_g, conv_ln_b, w_conv_out,
           w_attn_out, w_out, norm_ffn_g, w_gate_up, w_down, norm_ple_g, w_ple_gate, w_ple_proj, final_g):
    b, s, d = x.shape
    depth = w_in.shape[0]
    assert b == 1 and d == D_MODEL
    xs = x.reshape(s, d)
    row = lambda v: v.reshape(1, -1)
    o_q = 2 * CONV_CH
    o_f = o_q + 3 * ATTN_W
    o_g = o_f + N_HEADS

    for i in range(depth):
        wi = w_in[i]
        w_glu_a = wi[:, :CONV_CH].astype(BF16)
        w_glu_g = wi[:, CONV_CH:o_q].astype(BF16)
        w_qkv = wi[:, o_q:o_f].astype(BF16)
        w_f = jnp.pad(wi[:, o_f:o_g], ((0, 0), (0, LANES - N_HEADS))).astype(BF16)
        w_gates = wi[:, o_g:].astype(BF16)
        b_f = jnp.pad(b_forget[i], (0, LANES - N_HEADS)).reshape(1, LANES)
        g_mix = row(norm_mix_g[i])

        u = _norm_matmul(xs, g_mix, [w_glu_a, w_glu_g], [], _ep_glu, F32, tm=1024, tn=512, name="in_proj_glu")
        qkv = _norm_matmul(xs, g_mix, [w_qkv], [], _ep_identity, BF16, tm=1024, tn=1024, name="in_proj_qkv")
        gates = _norm_matmul(xs, g_mix, [w_gates], [], _ep_sigmoid, BF16, tm=1024, tn=1024, name="in_proj_gates")
        log_f = _norm_matmul(xs, g_mix, [w_f], [b_f], _ep_log_forget, F32, tm=1024, tn=LANES, name="in_proj_forget")

        c = _cumsum_rows(log_f)
        c_heads = c[:, :N_HEADS].T
        a_attn = _forgetting_attention(qkv, c_heads[:, :, None], c_heads[:, None, :])

        a_conv = _conformer_conv(u, conv_w[i], row(conv_b[i]), row(conv_ln_g[i]), row(conv_ln_b[i]))

        xs = _merge(xs, a_conv, a_attn, gates, w_conv_out[i].astype(BF16), w_attn_out[i].astype(BF16),
                    w_out[i].astype(BF16))
        xs = _ffn(xs, row(norm_ffn_g[i]), w_gate_up[i].astype(BF16), w_down[i].astype(BF16))
        xs = _ple(xs, row(norm_ple_g[i]), p[i, 0], w_ple_gate[i].astype(BF16), w_ple_proj[i].astype(BF16),
                  row(final_g), final=(i == depth - 1))
    return xs.reshape(b, s, d)
```

```python
import functools
import math

import jax
import jax.numpy as jnp
from jax import lax
from jax.experimental import pallas as pl
from jax.experimental.pallas import tpu as pltpu

D_MODEL = 2048
N_HEADS = 8
HEAD_DIM = 128
ATTN_W = N_HEADS * HEAD_DIM
CONV_CH = D_MODEL // 2
CONV_K = 31
FFN_HIDDEN = 5632
EPS = 1e-6
NEG_INF = -1e30
LOG2E = math.log2(math.e)

LANES = 128
SUBLANES = 8
MXU_DEPTH = 256
CONV_HALO = 32
VMEM_LIMIT_BYTES = 56 * 1024 * 1024

BF16 = jnp.bfloat16
F32 = jnp.float32


def _params(*sem):
    return pltpu.CompilerParams(dimension_semantics=sem, vmem_limit_bytes=VMEM_LIMIT_BYTES)


def _rms(x, g):
    ms = jnp.mean(x * x, axis=-1, keepdims=True)
    return x * lax.rsqrt(ms + EPS) * g


def _split3(x):
    hi = x.astype(BF16)
    r1 = x - hi.astype(F32)
    mid = r1.astype(BF16)
    lo = (r1 - mid.astype(F32)).astype(BF16)
    return hi, mid, lo


def _nmm_kernel(*refs, n_w, n_extra, epilogue):
    x_ref, g_ref = refs[0], refs[1]
    w_refs = refs[2:2 + n_w]
    e_refs = refs[2 + n_w:2 + n_w + n_extra]
    o_ref = refs[2 + n_w + n_extra]
    h_ref = refs[3 + n_w + n_extra]

    @pl.when(pl.program_id(1) == 0)
    def _():
        h_ref[...] = _rms(x_ref[...], g_ref[...]).astype(BF16)

    h = h_ref[...]
    accs = [jnp.dot(h, w[...], preferred_element_type=F32) for w in w_refs]
    o_ref[...] = epilogue(accs, [e[...] for e in e_refs]).astype(o_ref.dtype)


def _norm_matmul(x, g, ws, extras, epilogue, out_dtype, *, tm, tn, name):
    m, k = x.shape
    n = ws[0].shape[1]
    grid = (m // tm, n // tn)
    in_specs = [pl.BlockSpec((tm, k), lambda i, j: (i, 0)),
                pl.BlockSpec((1, k), lambda i, j: (0, 0))]
    in_specs += [pl.BlockSpec((k, tn), lambda i, j: (0, j)) for _ in ws]
    in_specs += [pl.BlockSpec((1, tn), lambda i, j: (0, j)) for _ in extras]
    return pl.pallas_call(
        functools.partial(_nmm_kernel, n_w=len(ws), n_extra=len(extras), epilogue=epilogue),
        grid=grid,
        in_specs=in_specs,
        out_specs=pl.BlockSpec((tm, tn), lambda i, j: (i, j)),
        out_shape=jax.ShapeDtypeStruct((m, n), out_dtype),
        scratch_shapes=[pltpu.VMEM((tm, k), BF16)],
        compiler_params=_params("parallel", "arbitrary"),
        name=name,
    )(x, g, *ws, *extras)


def _ep_scale(accs, extras):
    return accs[0] * extras[0]


def _ep_sigmoid(accs, extras):
    return jax.nn.sigmoid(accs[0])


def _ep_glu(accs, extras):
    return accs[0] * jax.nn.sigmoid(accs[1])


def _ep_log_forget(accs, extras):
    return jax.nn.log_sigmoid(accs[0] + extras[0])


def _cumsum_kernel(x_ref, hi_ref, mid_ref, lo_ref, carry_ref, *, tb):
    @pl.when(pl.program_id(0) == 0)
    def _():
        carry_ref[...] = jnp.zeros_like(carry_ref)

    row = lax.broadcasted_iota(jnp.int32, (tb, tb), 0)
    col = lax.broadcasted_iota(jnp.int32, (tb, tb), 1)
    tri = jnp.where(col <= row, 1.0, 0.0).astype(BF16)
    s = carry_ref[...]
    for piece in _split3(x_ref[...]):
        s = s + jnp.dot(tri, piece, preferred_element_type=F32)
    carry_ref[...] = s[tb - 1:tb, :]
    hi, mid, lo = _split3(s * LOG2E)
    hi_ref[...] = hi
    mid_ref[...] = mid
    lo_ref[...] = lo


def _cumsum_pieces(x, *, tb=256):
    m, n = x.shape
    spec = pl.BlockSpec((tb, n), lambda i: (i, 0))
    piece = jax.ShapeDtypeStruct((m, n), BF16)
    return pl.pallas_call(
        functools.partial(_cumsum_kernel, tb=tb),
        grid=(m // tb,),
        in_specs=[spec],
        out_specs=[spec, spec, spec],
        out_shape=[piece, piece, piece],
        scratch_shapes=[pltpu.VMEM((1, n), F32)],
        compiler_params=_params("arbitrary"),
        name="forget_cumsum",
    )(x)


def _conv_kernel(u_ref, halo_ref, w_ref, b_ref, lg_ref, lb_ref, o_ref, buf_ref, sh_ref, *, ts, rc):
    i = pl.program_id(0)
    halo = halo_ref[...]
    buf_ref[0:CONV_HALO, :] = jnp.where(i == 0, jnp.zeros_like(halo), halo)
    buf_ref[CONV_HALO:, :] = u_ref[...]
    span = ts + CONV_HALO - SUBLANES
    for r in range(1, SUBLANES):
        sh_ref[r - 1] = buf_ref[r:r + span, :]
    off = CONV_HALO - (CONV_K - 1)

    def chunk(c, carry):
        base = pl.multiple_of(c * rc, rc)
        acc = jnp.zeros((rc // SUBLANES, SUBLANES, CONV_CH), F32)
        for k in range(CONV_K):
            r = (off + k) % SUBLANES
            a = (off + k) - r
            win = (buf_ref[pl.ds(base + a, rc), :] if r == 0 else sh_ref[r - 1, pl.ds(base + a, rc), :])
            win = win.reshape(rc // SUBLANES, SUBLANES, CONV_CH)
            acc = acc + win * w_ref[k][None]
        acc = acc.reshape(rc, CONV_CH) + b_ref[...]
        mu = jnp.mean(acc, axis=-1, keepdims=True)
        d = acc - mu
        var = jnp.mean(d * d, axis=-1, keepdims=True)
        y = d * lax.rsqrt(var + EPS) * lg_ref[...] + lb_ref[...]
        o_ref[pl.ds(base, rc), :] = (y * jax.nn.sigmoid(y)).astype(o_ref.dtype)
        return carry

    lax.fori_loop(0, ts // rc, chunk, 0)


def _conformer_conv(u, conv_w, conv_b, ln_g, ln_b, *, ts=512, rc=16):
    s, c = u.shape
    halo_blocks = ts // CONV_HALO
    conv_w = jnp.broadcast_to(conv_w[:, None, :], (CONV_K, SUBLANES, c))
    return pl.pallas_call(
        functools.partial(_conv_kernel, ts=ts, rc=rc),
        grid=(s // ts,),
        in_specs=[pl.BlockSpec((ts, c), lambda i: (i, 0)),
                  pl.BlockSpec((CONV_HALO, c), lambda i: (jnp.maximum(i * halo_blocks - 1, 0), 0)),
                  pl.BlockSpec((CONV_K, SUBLANES, c), lambda i: (0, 0, 0)),
                  pl.BlockSpec((1, c), lambda i: (0, 0)),
                  pl.BlockSpec((1, c), lambda i: (0, 0)),
                  pl.BlockSpec((1, c), lambda i: (0, 0))],
        out_specs=pl.BlockSpec((ts, c), lambda i: (i, 0)),
        out_shape=jax.ShapeDtypeStruct((s, c), BF16),
        scratch_shapes=[pltpu.VMEM((ts + CONV_HALO, c), F32),
                        pltpu.VMEM((SUBLANES - 1, ts + CONV_HALO - SUBLANES, c), F32)],
        compiler_params=_params("parallel"),
        name="conformer_conv",
    )(u, u, conv_w, conv_b, ln_g, ln_b)


def _attn_kernel(qt_ref, k_ref, vt_ref, o_ref, m_ref, l_ref, acc_ref, *, tq, heads):
    qi = pl.program_id(1)
    m_ref[...] = jnp.full_like(m_ref, NEG_INF)
    l_ref[...] = jnp.zeros_like(l_ref)
    acc_ref[...] = jnp.zeros_like(acc_ref)

    def block(g, j, diagonal):
        s = jnp.dot(k_ref[g, j], qt_ref[g], preferred_element_type=F32)
        if diagonal:
            key = lax.broadcasted_iota(jnp.int32, (tq, tq), 0)
            qry = lax.broadcasted_iota(jnp.int32, (tq, tq), 1)
            s = jnp.where(key <= qry, s, NEG_INF)
        m_prev = m_ref[g]
        m_new = jnp.maximum(m_prev, jnp.max(s, axis=0, keepdims=True))
        alpha = jnp.exp2(m_prev - m_new)
        p = jnp.exp2(s - m_new)
        l_ref[g] = alpha * l_ref[g] + jnp.sum(p, axis=0, keepdims=True)
        acc_ref[g] = alpha * acc_ref[g] + jnp.dot(vt_ref[g, j], p.astype(BF16),
                                                  preferred_element_type=F32)
        m_ref[g] = m_new

    def body(j, carry):
        for g in range(heads):
            block(g, j, False)
        return carry

    lax.fori_loop(0, qi, body, 0)
    for g in range(heads):
        block(g, qi, True)
    for g in range(heads):
        o_ref[:, g * HEAD_DIM:(g + 1) * HEAD_DIM] = (acc_ref[g] / l_ref[g]).T.astype(o_ref.dtype)


def _forgetting_attention(q_aug_t, k_aug, v_t, *, tq, heads=2):
    h, _, s = q_aug_t.shape
    nk = s // tq
    return pl.pallas_call(
        functools.partial(_attn_kernel, tq=tq, heads=heads),
        grid=(h // heads, nk),
        in_specs=[pl.BlockSpec((heads, MXU_DEPTH, tq), lambda hg, qi: (hg, 0, qi)),
                  pl.BlockSpec((heads, nk, tq, MXU_DEPTH), lambda hg, qi: (hg, 0, 0, 0)),
                  pl.BlockSpec((heads, nk, HEAD_DIM, tq), lambda hg, qi: (hg, 0, 0, 0))],
        out_specs=pl.BlockSpec((tq, heads * HEAD_DIM), lambda hg, qi: (qi, hg)),
        out_shape=jax.ShapeDtypeStruct((s, h * HEAD_DIM), BF16),
        scratch_shapes=[pltpu.VMEM((heads, 1, tq), F32), pltpu.VMEM((heads, 1, tq), F32),
                        pltpu.VMEM((heads, HEAD_DIM, tq), F32)],
        compiler_params=_params("parallel", "arbitrary"),
        name="forgetting_attention",
    )(q_aug_t, k_aug, v_t)


def _attention_operands(qkv, c_pieces, *, tq):
    s = qkv.shape[0]
    nk = s // tq
    heads = lambda a: a.reshape(s, N_HEADS, HEAD_DIM)
    q = heads(qkv[:, :ATTN_W])
    k = heads(qkv[:, ATTN_W:2 * ATTN_W])
    v = heads(qkv[:, 2 * ATTN_W:])
    c = jnp.stack([piece[:, :N_HEADS] for piece in c_pieces], axis=-1)
    ones = jnp.ones_like(c)
    pad = jnp.zeros((s, N_HEADS, MXU_DEPTH - HEAD_DIM - 6), BF16)
    q_aug = jnp.concatenate([q, c, ones, pad], axis=-1)
    k_aug = jnp.concatenate([k, ones, -c, pad], axis=-1)
    q_aug_t = q_aug.transpose(1, 2, 0)
    k_aug = k_aug.transpose(1, 0, 2).reshape(N_HEADS, nk, tq, MXU_DEPTH)
    v_t = v.reshape(nk, tq, N_HEADS, HEAD_DIM).transpose(2, 0, 3, 1)
    return q_aug_t, k_aug, v_t


def _merge_kernel(x_ref, ac_ref, at_ref, gc_ref, ga_ref, wc_ref, wa_ref, wo_ref, o_ref):
    yc = jnp.dot(ac_ref[...], wc_ref[...], preferred_element_type=F32)
    ya = jnp.dot(at_ref[...], wa_ref[...], preferred_element_type=F32)
    merged = gc_ref[...].astype(F32) * yc + ga_ref[...].astype(F32) * ya
    o_ref[...] = x_ref[...] + jnp.dot(merged.astype(BF16), wo_ref[...], preferred_element_type=F32)


def _merge(x, a_conv, a_attn, gates, w_conv_out, w_attn_out, w_out, *, tm=256):
    s, d = x.shape
    const = lambda i: (0, 0)
    return pl.pallas_call(
        _merge_kernel,
        grid=(s // tm,),
        in_specs=[pl.BlockSpec((tm, d), lambda i: (i, 0)),
                  pl.BlockSpec((tm, CONV_CH), lambda i: (i, 0)),
                  pl.BlockSpec((tm, ATTN_W), lambda i: (i, 0)),
                  pl.BlockSpec((tm, d), lambda i: (i, 0)),
                  pl.BlockSpec((tm, d), lambda i: (i, 1)),
                  pl.BlockSpec((CONV_CH, d), const),
                  pl.BlockSpec((ATTN_W, d), const),
                  pl.BlockSpec((d, d), const)],
        out_specs=pl.BlockSpec((tm, d), lambda i: (i, 0)),
        out_shape=jax.ShapeDtypeStruct((s, d), F32),
        compiler_params=_params("parallel"),
        name="gated_merge",
    )(x, a_conv, a_attn, gates, gates, w_conv_out, w_attn_out, w_out)


def _ffn_kernel(x_ref, g_ref, wg_ref, wu_ref, wd_ref, o_ref, h_ref):
    @pl.when(pl.program_id(1) == 0)
    def _():
        x = x_ref[...]
        h_ref[...] = _rms(x, g_ref[...]).astype(BF16)
        o_ref[...] = x

    h = h_ref[...]
    gate = jnp.dot(h, wg_ref[...], preferred_element_type=F32)
    up = jnp.dot(h, wu_ref[...], preferred_element_type=F32)
    a = (gate * jax.nn.sigmoid(gate) * up).astype(BF16)
    o_ref[...] += jnp.dot(a, wd_ref[...], preferred_element_type=F32)


def _ffn(x, g, w_gate_up, w_down, *, tm=512, th=512):
    s, d = x.shape
    nh = FFN_HIDDEN // th
    return pl.pallas_call(
        _ffn_kernel,
        grid=(s // tm, nh),
        in_specs=[pl.BlockSpec((tm, d), lambda i, c: (i, 0)),
                  pl.BlockSpec((1, d), lambda i, c: (0, 0)),
                  pl.BlockSpec((d, th), lambda i, c: (0, c)),
                  pl.BlockSpec((d, th), lambda i, c: (0, c + nh)),
                  pl.BlockSpec((th, d), lambda i, c: (c, 0))],
        out_specs=pl.BlockSpec((tm, d), lambda i, c: (i, 0)),
        out_shape=jax.ShapeDtypeStruct((s, d), F32),
        scratch_shapes=[pltpu.VMEM((tm, d), BF16)],
        compiler_params=_params("parallel", "arbitrary"),
        name="swiglu_ffn",
    )(x, g, w_gate_up, w_gate_up, w_down)


def _ple_kernel(x_ref, g_ref, p_ref, wg_ref, wp_ref, fg_ref, o_ref, *, final):
    x = x_ref[...]
    h = _rms(x, g_ref[...]).astype(BF16)
    gate = jax.nn.sigmoid(jnp.dot(h, wg_ref[...], preferred_element_type=F32))
    emb = jnp.dot(p_ref[...].astype(BF16), wp_ref[...], preferred_element_type=F32)
    y = x + gate * emb
    if final:
        y = _rms(y, fg_ref[...])
    o_ref[...] = y


def _ple(x, g, p, w_gate, w_proj, final_g, *, final, tm=256):
    s, d = x.shape
    pd = p.shape[1]
    const = lambda i: (0, 0)
    return pl.pallas_call(
        functools.partial(_ple_kernel, final=final),
        grid=(s // tm,),
        in_specs=[pl.BlockSpec((tm, d), lambda i: (i, 0)),
                  pl.BlockSpec((1, d), const),
                  pl.BlockSpec((tm, pd), lambda i: (i, 0)),
                  pl.BlockSpec((d, d), const),
                  pl.BlockSpec((pd, d), const),
                  pl.BlockSpec((1, d), const)],
        out_specs=pl.BlockSpec((tm, d), lambda i: (i, 0)),
        out_shape=jax.ShapeDtypeStruct((s, d), F32),
        compiler_params=_params("parallel"),
        name="ple_final" if final else "ple",
    )(x, g, p, w_gate, w_proj, final_g)


def kernel(x, p, norm_mix_g, w_in, b_forget, conv_w, conv_b, conv_ln_g, conv_ln_b, w_conv_out,
           w_attn_out, w_out, norm_ffn_g, w_gate_up, w_down, norm_ple_g, w_ple_gate, w_ple_proj, final_g):
    b, s, d = x.shape
    depth = w_in.shape[0]
    assert b == 1 and d == D_MODEL
    xs = x.reshape(s, d)
    row = lambda v: v.reshape(1, -1)
    o_q = 2 * CONV_CH
    o_f = o_q + 3 * ATTN_W
    o_g = o_f + N_HEADS
    tq = 512
    qkv_scale = jnp.concatenate([jnp.full((1, ATTN_W), LOG2E / math.sqrt(HEAD_DIM), F32),
                                 jnp.ones((1, 2 * ATTN_W), F32)], axis=1)

    for i in range(depth):
        wi = w_in[i]
        w_glu_a = wi[:, :CONV_CH].astype(BF16)
        w_glu_g = wi[:, CONV_CH:o_q].astype(BF16)
        w_qkv = wi[:, o_q:o_f].astype(BF16)
        w_f = jnp.pad(wi[:, o_f:o_g], ((0, 0), (0, LANES - N_HEADS))).astype(BF16)
        w_gates = wi[:, o_g:].astype(BF16)
        b_f = jnp.pad(b_forget[i], (0, LANES - N_HEADS)).reshape(1, LANES)
        g_mix = row(norm_mix_g[i])

        u = _norm_matmul(xs, g_mix, [w_glu_a, w_glu_g], [], _ep_glu, F32, tm=1024, tn=512, name="in_proj_glu")
        qkv = _norm_matmul(xs, g_mix, [w_qkv], [qkv_scale], _ep_scale, BF16, tm=1024, tn=1024, name="in_proj_qkv")
        gates = _norm_matmul(xs, g_mix, [w_gates], [], _ep_sigmoid, BF16, tm=1024, tn=1024, name="in_proj_gates")
        log_f = _norm_matmul(xs, g_mix, [w_f], [b_f], _ep_log_forget, F32, tm=1024, tn=LANES, name="in_proj_forget")

        c_pieces = _cumsum_pieces(log_f)
        a_attn = _forgetting_attention(*_attention_operands(qkv, c_pieces, tq=tq), tq=tq)

        a_conv = _conformer_conv(u, conv_w[i], row(conv_b[i]), row(conv_ln_g[i]), row(conv_ln_b[i]))

        xs = _merge(xs, a_conv, a_attn, gates, w_conv_out[i].astype(BF16), w_attn_out[i].astype(BF16),
                    w_out[i].astype(BF16))
        xs = _ffn(xs, row(norm_ffn_g[i]), w_gate_up[i].astype(BF16), w_down[i].astype(BF16))
        xs = _ple(xs, row(norm_ple_g[i]), p[i, 0], w_ple_gate[i].astype(BF16), w_ple_proj[i].astype(BF16),
                  row(final_g), final=(i == depth - 1))
    return xs.reshape(b, s, d)
```

```python
import functools
import math

import jax
import jax.numpy as jnp
from jax import lax
from jax.experimental import pallas as pl
from jax.experimental.pallas import tpu as pltpu

D_MODEL = 2048
N_HEADS = 8
HEAD_DIM = 128
ATTN_W = N_HEADS * HEAD_DIM
CONV_CH = D_MODEL // 2
CONV_K = 31
FFN_HIDDEN = 5632
EPS = 1e-6
NEG_INF = -1e30
LOG2E = math.log2(math.e)

LANES = 128
SUBLANES = 8
MXU_DEPTH = 256
CONV_HALO = 32
VMEM_LIMIT_BYTES = 56 * 1024 * 1024

BF16 = jnp.bfloat16
F32 = jnp.float32


def _params(*sem):
    return pltpu.CompilerParams(dimension_semantics=sem, vmem_limit_bytes=VMEM_LIMIT_BYTES)


def _rms(x, g):
    ms = jnp.mean(x * x, axis=-1, keepdims=True)
    return x * lax.rsqrt(ms + EPS) * g


def _split3(x):
    hi = x.astype(BF16)
    r1 = x - hi.astype(F32)
    mid = r1.astype(BF16)
    lo = (r1 - mid.astype(F32)).astype(BF16)
    return hi, mid, lo


def _nmm_kernel(*refs, n_w, n_extra, epilogue):
    x_ref, g_ref = refs[0], refs[1]
    w_refs = refs[2:2 + n_w]
    e_refs = refs[2 + n_w:2 + n_w + n_extra]
    o_ref = refs[2 + n_w + n_extra]
    h_ref = refs[3 + n_w + n_extra]

    @pl.when(pl.program_id(1) == 0)
    def _():
        h_ref[...] = _rms(x_ref[...], g_ref[...]).astype(BF16)

    h = h_ref[...]
    accs = [jnp.dot(h, w[...], preferred_element_type=F32) for w in w_refs]
    o_ref[...] = epilogue(accs, [e[...] for e in e_refs]).astype(o_ref.dtype)


def _norm_matmul(x, g, w, col_tiles, extras, epilogue, out_dtype, *, n, tm, tn, name):
    m, k = x.shape
    grid = (m // tm, n // tn)
    in_specs = [pl.BlockSpec((tm, k), lambda i, j: (i, 0)),
                pl.BlockSpec((1, k), lambda i, j: (0, 0))]
    in_specs += [pl.BlockSpec((k, tn), functools.partial(lambda i, j, c: (0, j + c), c=c)) for c in col_tiles]
    in_specs += [pl.BlockSpec((1, tn), lambda i, j: (0, j)) for _ in extras]
    return pl.pallas_call(
        functools.partial(_nmm_kernel, n_w=len(col_tiles), n_extra=len(extras), epilogue=epilogue),
        grid=grid,
        in_specs=in_specs,
        out_specs=pl.BlockSpec((tm, tn), lambda i, j: (i, j)),
        out_shape=jax.ShapeDtypeStruct((m, n), out_dtype),
        scratch_shapes=[pltpu.VMEM((tm, k), BF16)],
        compiler_params=_params("parallel", "arbitrary"),
        name=name,
    )(x, g, *([w] * len(col_tiles)), *extras)


def _ep_sigmoid(accs, extras):
    return jax.nn.sigmoid(accs[0])


def _ep_glu(accs, extras):
    return accs[0] * jax.nn.sigmoid(accs[1])


def _ep_log_forget(accs, extras):
    return jax.nn.log_sigmoid(accs[0] + extras[0])


def _qkv_kernel(x_ref, g_ref, w_ref, qt_ref, k_ref, vt_ref, h_ref, *, tq, q_scale):
    j = pl.program_id(1)

    @pl.when(j == 0)
    def _():
        h_ref[...] = _rms(x_ref[...], g_ref[...]).astype(BF16)

    acc = jnp.dot(h_ref[...], w_ref[...], preferred_element_type=F32)
    head = lambda h, r: acc[r * tq:(r + 1) * tq, h * HEAD_DIM:(h + 1) * HEAD_DIM]
    halves = k_ref.shape[1]

    @pl.when(j == 0)
    def _():
        for h in range(N_HEADS):
            for r in range(halves):
                qt_ref[h, :, r * tq:(r + 1) * tq] = (head(h, r) * q_scale).T.astype(BF16)

    @pl.when(j == 1)
    def _():
        for h in range(N_HEADS):
            for r in range(halves):
                k_ref[h, r] = head(h, r).astype(BF16)

    @pl.when(j == 2)
    def _():
        for h in range(N_HEADS):
            for r in range(halves):
                vt_ref[h, r] = head(h, r).T.astype(BF16)


def _qkv_proj(x, g, w, col_tile, *, tm, tq):
    m, k = x.shape
    nk = m // tq
    halves = tm // tq
    q_scale = LOG2E / math.sqrt(HEAD_DIM)
    return pl.pallas_call(
        functools.partial(_qkv_kernel, tq=tq, q_scale=q_scale),
        grid=(m // tm, 3),
        in_specs=[pl.BlockSpec((tm, k), lambda i, j: (i, 0)),
                  pl.BlockSpec((1, k), lambda i, j: (0, 0)),
                  pl.BlockSpec((k, ATTN_W), lambda i, j: (0, j + col_tile))],
        out_specs=[pl.BlockSpec((N_HEADS, HEAD_DIM, tm), lambda i, j: (0, 0, i)),
                   pl.BlockSpec((N_HEADS, halves, tq, HEAD_DIM), lambda i, j: (0, i, 0, 0)),
                   pl.BlockSpec((N_HEADS, halves, HEAD_DIM, tq), lambda i, j: (0, i, 0, 0))],
        out_shape=[jax.ShapeDtypeStruct((N_HEADS, HEAD_DIM, m), BF16),
                   jax.ShapeDtypeStruct((N_HEADS, nk, tq, HEAD_DIM), BF16),
                   jax.ShapeDtypeStruct((N_HEADS, nk, HEAD_DIM, tq), BF16)],
        scratch_shapes=[pltpu.VMEM((tm, k), BF16)],
        compiler_params=_params("parallel", "arbitrary"),
        name="in_proj_qkv",
    )(x, g, w)


def _cumsum_kernel(x_ref, qc_ref, kc_ref, carry_ref, *, tb):
    @pl.when(pl.program_id(0) == 0)
    def _():
        carry_ref[...] = jnp.zeros_like(carry_ref)

    row = lax.broadcasted_iota(jnp.int32, (tb, tb), 0)
    col = lax.broadcasted_iota(jnp.int32, (tb, tb), 1)
    tri = jnp.where(col <= row, 1.0, 0.0).astype(BF16)
    s = carry_ref[...]
    for limb in _split3(x_ref[...]):
        s = s + jnp.dot(tri, limb, preferred_element_type=F32)
    carry_ref[...] = s[tb - 1:tb, :]
    c2 = s * LOG2E
    limbs_col = jnp.concatenate(_split3(c2), axis=1)
    limbs_row = [limb.astype(F32) for limb in _split3(c2.T)]

    sel_r = lax.broadcasted_iota(jnp.int32, (3 * LANES, HEAD_DIM), 0)
    sel_c = lax.broadcasted_iota(jnp.int32, (3 * LANES, HEAD_DIM), 1)
    lane = lax.broadcasted_iota(jnp.int32, (tb, HEAD_DIM), 1)
    ones_cols = jnp.where(lane < 3, 1.0, 0.0)
    sub = lax.broadcasted_iota(jnp.int32, (HEAD_DIM, tb), 0)
    ones_rows = jnp.where(sub < 3, 0.0, jnp.where(sub < 6, 1.0, 0.0))
    for h in range(N_HEADS):
        target = jnp.where(sel_c < 3, -1, jnp.where(sel_c < 6, (sel_c - 3) * LANES + h, -1))
        pick = jnp.where(sel_r == target, -1.0, 0.0).astype(BF16)
        kc = jnp.dot(limbs_col, pick, preferred_element_type=F32) + ones_cols
        kc_ref[h, 0] = kc.astype(BF16)
        qc = ones_rows
        for l in range(3):
            qc = jnp.where(sub == l, limbs_row[l][h:h + 1, :], qc)
        qc_ref[h] = qc.astype(BF16)


def _cumsum_operands(x, *, tb):
    m, n = x.shape
    return pl.pallas_call(
        functools.partial(_cumsum_kernel, tb=tb),
        grid=(m // tb,),
        in_specs=[pl.BlockSpec((tb, n), lambda i: (i, 0))],
        out_specs=[pl.BlockSpec((N_HEADS, HEAD_DIM, tb), lambda i: (0, 0, i)),
                   pl.BlockSpec((N_HEADS, 1, tb, HEAD_DIM), lambda i: (0, i, 0, 0))],
        out_shape=[jax.ShapeDtypeStruct((N_HEADS, HEAD_DIM, m), BF16),
                   jax.ShapeDtypeStruct((N_HEADS, m // tb, tb, HEAD_DIM), BF16)],
        scratch_shapes=[pltpu.VMEM((1, n), F32)],
        compiler_params=_params("arbitrary"),
        name="forget_cumsum",
    )(x)


def _conv_kernel(u_ref, halo_ref, w_ref, b_ref, lg_ref, lb_ref, o_ref, buf_ref, sh_ref, *, ts, rc):
    i = pl.program_id(0)
    halo = halo_ref[...]
    buf_ref[0:CONV_HALO, :] = jnp.where(i == 0, jnp.zeros_like(halo), halo)
    buf_ref[CONV_HALO:, :] = u_ref[...]
    span = ts + CONV_HALO - SUBLANES
    for r in range(1, SUBLANES):
        sh_ref[r - 1] = buf_ref[r:r + span, :]
    off = CONV_HALO - (CONV_K - 1)

    def chunk(c, carry):
        base = pl.multiple_of(c * rc, rc)
        acc = jnp.zeros((rc // SUBLANES, SUBLANES, CONV_CH), F32)
        for k in range(CONV_K):
            r = (off + k) % SUBLANES
            a = (off + k) - r
            win = (buf_ref[pl.ds(base + a, rc), :] if r == 0 else sh_ref[r - 1, pl.ds(base + a, rc), :])
            win = win.reshape(rc // SUBLANES, SUBLANES, CONV_CH)
            acc = acc + win * w_ref[k][None]
        acc = acc.reshape(rc, CONV_CH) + b_ref[...]
        mu = jnp.mean(acc, axis=-1, keepdims=True)
        d = acc - mu
        var = jnp.mean(d * d, axis=-1, keepdims=True)
        y = d * lax.rsqrt(var + EPS) * lg_ref[...] + lb_ref[...]
        o_ref[pl.ds(base, rc), :] = (y * jax.nn.sigmoid(y)).astype(o_ref.dtype)
        return carry

    lax.fori_loop(0, ts // rc, chunk, 0)


def _conformer_conv(u, conv_w, conv_b, ln_g, ln_b, *, ts=512, rc=16):
    s, c = u.shape
    halo_blocks = ts // CONV_HALO
    conv_w = jnp.broadcast_to(conv_w[:, None, :], (CONV_K, SUBLANES, c))
    return pl.pallas_call(
        functools.partial(_conv_kernel, ts=ts, rc=rc),
        grid=(s // ts,),
        in_specs=[pl.BlockSpec((ts, c), lambda i: (i, 0)),
                  pl.BlockSpec((CONV_HALO, c), lambda i: (jnp.maximum(i * halo_blocks - 1, 0), 0)),
                  pl.BlockSpec((CONV_K, SUBLANES, c), lambda i: (0, 0, 0)),
                  pl.BlockSpec((1, c), lambda i: (0, 0)),
                  pl.BlockSpec((1, c), lambda i: (0, 0)),
                  pl.BlockSpec((1, c), lambda i: (0, 0))],
        out_specs=pl.BlockSpec((ts, c), lambda i: (i, 0)),
        out_shape=jax.ShapeDtypeStruct((s, c), BF16),
        scratch_shapes=[pltpu.VMEM((ts + CONV_HALO, c), F32),
                        pltpu.VMEM((SUBLANES - 1, ts + CONV_HALO - SUBLANES, c), F32)],
        compiler_params=_params("parallel"),
        name="conformer_conv",
    )(u, u, conv_w, conv_b, ln_g, ln_b)


def _attn_kernel(qt_ref, qc_ref, k_ref, kc_ref, vt_ref, o_ref, qa_ref, ka_ref, m_ref, l_ref, acc_ref,
                 *, tq, heads):
    qi = pl.program_id(1)
    nk = k_ref.shape[1]

    @pl.when(qi == 0)
    def _():
        def widen(j, carry):
            for g in range(heads):
                ka_ref[g, j, :, :HEAD_DIM] = k_ref[g, j]
                ka_ref[g, j, :, HEAD_DIM:] = kc_ref[g, j]
            return carry
        lax.fori_loop(0, nk, widen, 0)

    qa_ref[:, :HEAD_DIM, :] = qt_ref[...]
    qa_ref[:, HEAD_DIM:, :] = qc_ref[...]
    m_ref[...] = jnp.full_like(m_ref, NEG_INF)
    l_ref[...] = jnp.zeros_like(l_ref)
    acc_ref[...] = jnp.zeros_like(acc_ref)

    def block(g, j, diagonal):
        s = jnp.dot(ka_ref[g, j], qa_ref[g], preferred_element_type=F32)
        if diagonal:
            key = lax.broadcasted_iota(jnp.int32, (tq, tq), 0)
            qry = lax.broadcasted_iota(jnp.int32, (tq, tq), 1)
            s = jnp.where(key <= qry, s, NEG_INF)
        m_prev = m_ref[g]
        m_new = jnp.maximum(m_prev, jnp.max(s, axis=0, keepdims=True))
        alpha = jnp.exp2(m_prev - m_new)
        p = jnp.exp2(s - m_new)
        l_ref[g] = alpha * l_ref[g] + jnp.sum(p, axis=0, keepdims=True)
        acc_ref[g] = alpha * acc_ref[g] + jnp.dot(vt_ref[g, j], p.astype(BF16),
                                                  preferred_element_type=F32)
        m_ref[g] = m_new

    def body(j, carry):
        for g in range(heads):
            block(g, j, False)
        return carry

    lax.fori_loop(0, qi, body, 0)
    for g in range(heads):
        block(g, qi, True)
    for g in range(heads):
        o_ref[:, g * HEAD_DIM:(g + 1) * HEAD_DIM] = (acc_ref[g] / l_ref[g]).T.astype(o_ref.dtype)


def _forgetting_attention(q_t, qc, k, kc, v_t, *, tq, heads=2):
    h, _, s = q_t.shape
    nk = s // tq
    q_spec = pl.BlockSpec((heads, HEAD_DIM, tq), lambda hg, qi: (hg, 0, qi))
    k_spec = pl.BlockSpec((heads, nk, tq, HEAD_DIM), lambda hg, qi: (hg, 0, 0, 0))
    return pl.pallas_call(
        functools.partial(_attn_kernel, tq=tq, heads=heads),
        grid=(h // heads, nk),
        in_specs=[q_spec, q_spec, k_spec, k_spec,
                  pl.BlockSpec((heads, nk, HEAD_DIM, tq), lambda hg, qi: (hg, 0, 0, 0))],
        out_specs=pl.BlockSpec((tq, heads * HEAD_DIM), lambda hg, qi: (qi, hg)),
        out_shape=jax.ShapeDtypeStruct((s, h * HEAD_DIM), BF16),
        scratch_shapes=[pltpu.VMEM((heads, MXU_DEPTH, tq), BF16),
                        pltpu.VMEM((heads, nk, tq, MXU_DEPTH), BF16),
                        pltpu.VMEM((heads, 1, tq), F32), pltpu.VMEM((heads, 1, tq), F32),
                        pltpu.VMEM((heads, HEAD_DIM, tq), F32)],
        compiler_params=_params("parallel", "arbitrary"),
        name="forgetting_attention",
    )(q_t, qc, k, kc, v_t)


def _merge_kernel(x_ref, ac_ref, at_ref, gc_ref, ga_ref, wc_ref, wa_ref, wo_ref, o_ref):
    yc = jnp.dot(ac_ref[...], wc_ref[...], preferred_element_type=F32)
    ya = jnp.dot(at_ref[...], wa_ref[...], preferred_element_type=F32)
    merged = gc_ref[...].astype(F32) * yc + ga_ref[...].astype(F32) * ya
    o_ref[...] = x_ref[...] + jnp.dot(merged.astype(BF16), wo_ref[...], preferred_element_type=F32)


def _merge(x, a_conv, a_attn, gates, w_conv_out, w_attn_out, w_out, *, tm=256):
    s, d = x.shape
    const = lambda i: (0, 0)
    return pl.pallas_call(
        _merge_kernel,
        grid=(s // tm,),
        in_specs=[pl.BlockSpec((tm, d), lambda i: (i, 0)),
                  pl.BlockSpec((tm, CONV_CH), lambda i: (i, 0)),
                  pl.BlockSpec((tm, ATTN_W), lambda i: (i, 0)),
                  pl.BlockSpec((tm, d), lambda i: (i, 0)),
                  pl.BlockSpec((tm, d), lambda i: (i, 1)),
                  pl.BlockSpec((CONV_CH, d), const),
                  pl.BlockSpec((ATTN_W, d), const),
                  pl.BlockSpec((d, d), const)],
        out_specs=pl.BlockSpec((tm, d), lambda i: (i, 0)),
        out_shape=jax.ShapeDtypeStruct((s, d), F32),
        compiler_params=_params("parallel"),
        name="gated_merge",
    )(x, a_conv, a_attn, gates, gates, w_conv_out, w_attn_out, w_out)


def _ffn_kernel(x_ref, g_ref, wg_ref, wu_ref, wd_ref, o_ref, h_ref):
    @pl.when(pl.program_id(1) == 0)
    def _():
        x = x_ref[...]
        h_ref[...] = _rms(x, g_ref[...]).astype(BF16)
        o_ref[...] = x

    h = h_ref[...]
    gate = jnp.dot(h, wg_ref[...], preferred_element_type=F32)
    up = jnp.dot(h, wu_ref[...], preferred_element_type=F32)
    a = (gate * jax.nn.sigmoid(gate) * up).astype(BF16)
    o_ref[...] += jnp.dot(a, wd_ref[...], preferred_element_type=F32)


def _ffn(x, g, w_gate_up, w_down, *, tm=512, th=512):
    s, d = x.shape
    nh = FFN_HIDDEN // th
    return pl.pallas_call(
        _ffn_kernel,
        grid=(s // tm, nh),
        in_specs=[pl.BlockSpec((tm, d), lambda i, c: (i, 0)),
                  pl.BlockSpec((1, d), lambda i, c: (0, 0)),
                  pl.BlockSpec((d, th), lambda i, c: (0, c)),
                  pl.BlockSpec((d, th), lambda i, c: (0, c + nh)),
                  pl.BlockSpec((th, d), lambda i, c: (c, 0))],
        out_specs=pl.BlockSpec((tm, d), lambda i, c: (i, 0)),
        out_shape=jax.ShapeDtypeStruct((s, d), F32),
        scratch_shapes=[pltpu.VMEM((tm, d), BF16)],
        compiler_params=_params("parallel", "arbitrary"),
        name="swiglu_ffn",
    )(x, g, w_gate_up, w_gate_up, w_down)


def _ple_kernel(x_ref, g_ref, p_ref, wg_ref, wp_ref, fg_ref, o_ref, *, final):
    x = x_ref[...]
    h = _rms(x, g_ref[...]).astype(BF16)
    gate = jax.nn.sigmoid(jnp.dot(h, wg_ref[...], preferred_element_type=F32))
    emb = jnp.dot(p_ref[...].astype(BF16), wp_ref[...], preferred_element_type=F32)
    y = x + gate * emb
    if final:
        y = _rms(y, fg_ref[...])
    o_ref[...] = y


def _ple(x, g, p, w_gate, w_proj, final_g, *, final, tm=256):
    s, d = x.shape
    pd = p.shape[1]
    const = lambda i: (0, 0)
    return pl.pallas_call(
        functools.partial(_ple_kernel, final=final),
        grid=(s // tm,),
        in_specs=[pl.BlockSpec((tm, d), lambda i: (i, 0)),
                  pl.BlockSpec((1, d), const),
                  pl.BlockSpec((tm, pd), lambda i: (i, 0)),
                  pl.BlockSpec((d, d), const),
                  pl.BlockSpec((pd, d), const),
                  pl.BlockSpec((1, d), const)],
        out_specs=pl.BlockSpec((tm, d), lambda i: (i, 0)),
        out_shape=jax.ShapeDtypeStruct((s, d), F32),
        compiler_params=_params("parallel"),
        name="ple_final" if final else "ple",
    )(x, g, p, w_gate, w_proj, final_g)


def kernel(x, p, norm_mix_g, w_in, b_forget, conv_w, conv_b, conv_ln_g, conv_ln_b, w_conv_out,
           w_attn_out, w_out, norm_ffn_g, w_gate_up, w_down, norm_ple_g, w_ple_gate, w_ple_proj, final_g):
    b, s, d = x.shape
    depth = w_in.shape[0]
    assert b == 1 and d == D_MODEL
    xs = x.reshape(s, d)
    row = lambda v: v.reshape(1, -1)
    o_f = 2 * CONV_CH + 3 * ATTN_W
    o_g = o_f + N_HEADS
    tq = 512

    for i in range(depth):
        wi = w_in[i]
        w_perm = jnp.concatenate([wi[:, :o_f], wi[:, o_g:], wi[:, o_f:o_g],
                                  jnp.zeros((d, LANES - N_HEADS), F32)], axis=1).astype(BF16)
        b_f = jnp.pad(b_forget[i], (0, LANES - N_HEADS)).reshape(1, LANES)
        g_mix = row(norm_mix_g[i])
        c_glu, c_qkv, c_gates, c_f = 0, 2 * CONV_CH, o_f, o_f + 2 * D_MODEL

        u = _norm_matmul(xs, g_mix, w_perm, [c_glu // 512, (c_glu + CONV_CH) // 512], [], _ep_glu, F32,
                         n=CONV_CH, tm=1024, tn=512, name="in_proj_glu")
        q_t, k, v_t = _qkv_proj(xs, g_mix, w_perm, c_qkv // ATTN_W, tm=1024, tq=tq)
        gates = _norm_matmul(xs, g_mix, w_perm, [c_gates // 1024], [], _ep_sigmoid, BF16,
                             n=2 * D_MODEL, tm=1024, tn=1024, name="in_proj_gates")
        log_f = _norm_matmul(xs, g_mix, w_perm, [c_f // LANES], [b_f], _ep_log_forget, F32,
                             n=LANES, tm=1024, tn=LANES, name="in_proj_forget")

        qc, kc = _cumsum_operands(log_f, tb=tq)
        a_attn = _forgetting_attention(q_t, qc, k, kc, v_t, tq=tq)

        a_conv = _conformer_conv(u, conv_w[i], row(conv_b[i]), row(conv_ln_g[i]), row(conv_ln_b[i]))

        xs = _merge(xs, a_conv, a_attn, gates, w_conv_out[i].astype(BF16), w_attn_out[i].astype(BF16),
                    w_out[i].astype(BF16))
        xs = _ffn(xs, row(norm_ffn_g[i]), w_gate_up[i].astype(BF16), w_down[i].astype(BF16))
        xs = _ple(xs, row(norm_ple_g[i]), p[i, 0], w_ple_gate[i].astype(BF16), w_ple_proj[i].astype(BF16),
                  row(final_g), final=(i == depth - 1))
    return xs.reshape(b, s, d)
```

```python
import functools
import math

import jax
import jax.numpy as jnp
from jax import lax
from jax.experimental import pallas as pl
from jax.experimental.pallas import tpu as pltpu

D_MODEL = 2048
N_HEADS = 8
HEAD_DIM = 128
ATTN_W = N_HEADS * HEAD_DIM
CONV_CH = D_MODEL // 2
CONV_K = 31
FFN_HIDDEN = 5632
EPS = 1e-6
NEG_INF = -1e30
LOG2E = math.log2(math.e)

LANES = 128
SUBLANES = 8
MXU_DEPTH = 256
CONV_HALO = 32
VMEM_LIMIT_BYTES = 56 * 1024 * 1024

BF16 = jnp.bfloat16
F32 = jnp.float32


def _params(*sem):
    return pltpu.CompilerParams(dimension_semantics=sem, vmem_limit_bytes=VMEM_LIMIT_BYTES)


def _rms(x, g):
    ms = jnp.mean(x * x, axis=-1, keepdims=True)
    return x * lax.rsqrt(ms + EPS) * g


def _split3(x):
    hi = x.astype(BF16)
    r1 = x - hi.astype(F32)
    mid = r1.astype(BF16)
    lo = (r1 - mid.astype(F32)).astype(BF16)
    return hi, mid, lo


def _nmm_kernel(*refs, n_w, n_extra, epilogue):
    x_ref, g_ref = refs[0], refs[1]
    w_refs = refs[2:2 + n_w]
    e_refs = refs[2 + n_w:2 + n_w + n_extra]
    o_ref = refs[2 + n_w + n_extra]
    h_ref = refs[3 + n_w + n_extra]

    @pl.when(pl.program_id(1) == 0)
    def _():
        h_ref[...] = _rms(x_ref[...], g_ref[...]).astype(BF16)

    h = h_ref[...]
    accs = [jnp.dot(h, w[...], preferred_element_type=F32) for w in w_refs]
    o_ref[...] = epilogue(accs, [e[...] for e in e_refs]).astype(o_ref.dtype)


def _norm_matmul(x, g, w, col_tiles, extras, epilogue, out_dtype, *, n, tm, tn, name):
    m, k = x.shape
    grid = (m // tm, n // tn)
    in_specs = [pl.BlockSpec((tm, k), lambda i, j: (i, 0)),
                pl.BlockSpec((1, k), lambda i, j: (0, 0))]
    in_specs += [pl.BlockSpec((k, tn), functools.partial(lambda i, j, c: (0, j + c), c=c)) for c in col_tiles]
    in_specs += [pl.BlockSpec((1, tn), lambda i, j: (0, j)) for _ in extras]
    return pl.pallas_call(
        functools.partial(_nmm_kernel, n_w=len(col_tiles), n_extra=len(extras), epilogue=epilogue),
        grid=grid,
        in_specs=in_specs,
        out_specs=pl.BlockSpec((tm, tn), lambda i, j: (i, j)),
        out_shape=jax.ShapeDtypeStruct((m, n), out_dtype),
        scratch_shapes=[pltpu.VMEM((tm, k), BF16)],
        compiler_params=_params("parallel", "arbitrary"),
        name=name,
    )(x, g, *([w] * len(col_tiles)), *extras)


def _ep_sigmoid(accs, extras):
    return jax.nn.sigmoid(accs[0])


def _ep_glu(accs, extras):
    return accs[0] * jax.nn.sigmoid(accs[1])


def _ep_log_forget(accs, extras):
    return jax.nn.log_sigmoid(accs[0] + extras[0])


def _qkv_kernel(x_ref, g_ref, w_ref, qt_ref, k_ref, vt_ref, h_ref, *, tq, q_scale):
    j = pl.program_id(1)

    @pl.when(j == 0)
    def _():
        h_ref[...] = _rms(x_ref[...], g_ref[...]).astype(BF16)

    acc = jnp.dot(h_ref[...], w_ref[...], preferred_element_type=F32)
    head = lambda h, r: acc[r * tq:(r + 1) * tq, h * HEAD_DIM:(h + 1) * HEAD_DIM]
    halves = k_ref.shape[1]

    @pl.when(j == 0)
    def _():
        for h in range(N_HEADS):
            for r in range(halves):
                qt_ref[h, :, r * tq:(r + 1) * tq] = (head(h, r) * q_scale).T.astype(BF16)

    @pl.when(j == 1)
    def _():
        for h in range(N_HEADS):
            for r in range(halves):
                k_ref[h, r] = head(h, r).astype(BF16)

    @pl.when(j == 2)
    def _():
        for h in range(N_HEADS):
            for r in range(halves):
                vt_ref[h, r] = head(h, r).T.astype(BF16)


def _qkv_proj(x, g, w, col_tile, *, tm, tq):
    m, k = x.shape
    nk = m // tq
    halves = tm // tq
    q_scale = LOG2E / math.sqrt(HEAD_DIM)
    return pl.pallas_call(
        functools.partial(_qkv_kernel, tq=tq, q_scale=q_scale),
        grid=(m // tm, 3),
        in_specs=[pl.BlockSpec((tm, k), lambda i, j: (i, 0)),
                  pl.BlockSpec((1, k), lambda i, j: (0, 0)),
                  pl.BlockSpec((k, ATTN_W), lambda i, j: (0, j + col_tile))],
        out_specs=[pl.BlockSpec((N_HEADS, HEAD_DIM, tm), lambda i, j: (0, 0, i)),
                   pl.BlockSpec((N_HEADS, halves, tq, HEAD_DIM), lambda i, j: (0, i, 0, 0)),
                   pl.BlockSpec((N_HEADS, halves, HEAD_DIM, tq), lambda i, j: (0, i, 0, 0))],
        out_shape=[jax.ShapeDtypeStruct((N_HEADS, HEAD_DIM, m), BF16),
                   jax.ShapeDtypeStruct((N_HEADS, nk, tq, HEAD_DIM), BF16),
                   jax.ShapeDtypeStruct((N_HEADS, nk, HEAD_DIM, tq), BF16)],
        scratch_shapes=[pltpu.VMEM((tm, k), BF16)],
        compiler_params=_params("parallel", "arbitrary"),
        name="in_proj_qkv",
    )(x, g, w)


def _cumsum_kernel(x_ref, qc_ref, kc_ref, carry_ref, *, tb):
    @pl.when(pl.program_id(0) == 0)
    def _():
        carry_ref[...] = jnp.zeros_like(carry_ref)

    row = lax.broadcasted_iota(jnp.int32, (tb, tb), 0)
    col = lax.broadcasted_iota(jnp.int32, (tb, tb), 1)
    tri = jnp.where(col <= row, 1.0, 0.0).astype(BF16)
    s = carry_ref[...]
    for limb in _split3(x_ref[...]):
        s = s + jnp.dot(tri, limb, preferred_element_type=F32)
    carry_ref[...] = s[tb - 1:tb, :]
    c2 = s * LOG2E
    limbs_col = jnp.concatenate(_split3(c2), axis=1)
    limbs_row = [limb.astype(F32) for limb in _split3(c2.T)]

    sel_r = lax.broadcasted_iota(jnp.int32, (3 * LANES, HEAD_DIM), 0)
    sel_c = lax.broadcasted_iota(jnp.int32, (3 * LANES, HEAD_DIM), 1)
    lane = lax.broadcasted_iota(jnp.int32, (tb, HEAD_DIM), 1)
    ones_cols = jnp.where(lane < 3, 1.0, 0.0)
    sub = lax.broadcasted_iota(jnp.int32, (HEAD_DIM, tb), 0)
    ones_rows = jnp.where(sub < 3, 0.0, jnp.where(sub < 6, 1.0, 0.0))
    for h in range(N_HEADS):
        target = jnp.where(sel_c < 3, -1, jnp.where(sel_c < 6, (sel_c - 3) * LANES + h, -1))
        pick = jnp.where(sel_r == target, -1.0, 0.0).astype(BF16)
        kc = jnp.dot(limbs_col, pick, preferred_element_type=F32) + ones_cols
        kc_ref[h, 0] = kc.astype(BF16)
        qc = ones_rows
        for l in range(3):
            qc = jnp.where(sub == l, limbs_row[l][h:h + 1, :], qc)
        qc_ref[h] = qc.astype(BF16)


def _cumsum_operands(x, *, tb):
    m, n = x.shape
    return pl.pallas_call(
        functools.partial(_cumsum_kernel, tb=tb),
        grid=(m // tb,),
        in_specs=[pl.BlockSpec((tb, n), lambda i: (i, 0))],
        out_specs=[pl.BlockSpec((N_HEADS, HEAD_DIM, tb), lambda i: (0, 0, i)),
                   pl.BlockSpec((N_HEADS, 1, tb, HEAD_DIM), lambda i: (0, i, 0, 0))],
        out_shape=[jax.ShapeDtypeStruct((N_HEADS, HEAD_DIM, m), BF16),
                   jax.ShapeDtypeStruct((N_HEADS, m // tb, tb, HEAD_DIM), BF16)],
        scratch_shapes=[pltpu.VMEM((1, n), F32)],
        compiler_params=_params("arbitrary"),
        name="forget_cumsum",
    )(x)


def _conv_kernel(u_ref, halo_ref, w_ref, b_ref, lg_ref, lb_ref, o_ref, buf_ref, sh_ref, *, ts, rc):
    i = pl.program_id(0)
    halo = halo_ref[...]
    buf_ref[0:CONV_HALO, :] = jnp.where(i == 0, jnp.zeros_like(halo), halo)
    buf_ref[CONV_HALO:, :] = u_ref[...]
    span = ts + CONV_HALO - SUBLANES
    for r in range(1, SUBLANES):
        sh_ref[r - 1] = buf_ref[r:r + span, :]
    off = CONV_HALO - (CONV_K - 1)

    def chunk(c, carry):
        base = pl.multiple_of(c * rc, rc)
        acc = jnp.zeros((rc // SUBLANES, SUBLANES, CONV_CH), F32)
        for k in range(CONV_K):
            r = (off + k) % SUBLANES
            a = (off + k) - r
            win = (buf_ref[pl.ds(base + a, rc), :] if r == 0 else sh_ref[r - 1, pl.ds(base + a, rc), :])
            win = win.reshape(rc // SUBLANES, SUBLANES, CONV_CH)
            acc = acc + win * w_ref[k][None]
        acc = acc.reshape(rc, CONV_CH) + b_ref[...]
        mu = jnp.mean(acc, axis=-1, keepdims=True)
        d = acc - mu
        var = jnp.mean(d * d, axis=-1, keepdims=True)
        y = d * lax.rsqrt(var + EPS) * lg_ref[...] + lb_ref[...]
        o_ref[pl.ds(base, rc), :] = (y * jax.nn.sigmoid(y)).astype(o_ref.dtype)
        return carry

    lax.fori_loop(0, ts // rc, chunk, 0)


def _conformer_conv(u, conv_w, conv_b, ln_g, ln_b, *, ts=512, rc=16):
    s, c = u.shape
    halo_blocks = ts // CONV_HALO
    conv_w = jnp.broadcast_to(conv_w[:, None, :], (CONV_K, SUBLANES, c))
    return pl.pallas_call(
        functools.partial(_conv_kernel, ts=ts, rc=rc),
        grid=(s // ts,),
        in_specs=[pl.BlockSpec((ts, c), lambda i: (i, 0)),
                  pl.BlockSpec((CONV_HALO, c), lambda i: (jnp.maximum(i * halo_blocks - 1, 0), 0)),
                  pl.BlockSpec((CONV_K, SUBLANES, c), lambda i: (0, 0, 0)),
                  pl.BlockSpec((1, c), lambda i: (0, 0)),
                  pl.BlockSpec((1, c), lambda i: (0, 0)),
                  pl.BlockSpec((1, c), lambda i: (0, 0))],
        out_specs=pl.BlockSpec((ts, c), lambda i: (i, 0)),
        out_shape=jax.ShapeDtypeStruct((s, c), BF16),
        scratch_shapes=[pltpu.VMEM((ts + CONV_HALO, c), F32),
                        pltpu.VMEM((SUBLANES - 1, ts + CONV_HALO - SUBLANES, c), F32)],
        compiler_params=_params("parallel"),
        name="conformer_conv",
    )(u, u, conv_w, conv_b, ln_g, ln_b)


def _attn_kernel(qt_ref, qc_ref, k_ref, kc_ref, vt_ref, o_ref, qa_ref, ka_ref, m_ref, l_ref, acc_ref,
                 s0_ref, s1_ref, p0_ref, p1_ref, a0_ref, a1_ref, *, tq, heads):
    qi = pl.program_id(1)
    nk = k_ref.shape[1]
    s_refs, p_refs, a_refs = (s0_ref, s1_ref), (p0_ref, p1_ref), (a0_ref, a1_ref)

    @pl.when(qi == 0)
    def _():
        def widen(j, carry):
            for g in range(heads):
                ka_ref[g, j, :, :HEAD_DIM] = k_ref[g, j]
                ka_ref[g, j, :, HEAD_DIM:] = kc_ref[g, j]
            return carry
        lax.fori_loop(0, nk, widen, 0)

    qa_ref[:, :HEAD_DIM, :] = qt_ref[...]
    qa_ref[:, HEAD_DIM:, :] = qc_ref[...]
    m_ref[...] = jnp.full_like(m_ref, NEG_INF)
    l_ref[...] = jnp.zeros_like(l_ref)
    acc_ref[...] = jnp.zeros_like(acc_ref)
    for slot in range(2):
        p_refs[slot][...] = jnp.zeros_like(p_refs[slot])
        a_refs[slot][...] = jnp.ones_like(a_refs[slot])

    def scores(g, j, slot):
        s_refs[slot][g] = jnp.dot(ka_ref[g, j], qa_ref[g], preferred_element_type=F32)

    def weighted_values(g, j, slot):
        acc_ref[g] = a_refs[slot][g] * acc_ref[g] + jnp.dot(vt_ref[g, j], p_refs[slot][g],
                                                            preferred_element_type=F32)

    def softmax(g, slot, diagonal):
        s = s_refs[slot][g]
        if diagonal:
            key = lax.broadcasted_iota(jnp.int32, (tq, tq), 0)
            qry = lax.broadcasted_iota(jnp.int32, (tq, tq), 1)
            s = jnp.where(key <= qry, s, NEG_INF)
        m_prev = m_ref[g]
        m_new = jnp.maximum(m_prev, jnp.max(s, axis=0, keepdims=True))
        alpha = jnp.exp2(m_prev - m_new)
        p = jnp.exp2(s - m_new)
        l_ref[g] = alpha * l_ref[g] + jnp.sum(p, axis=0, keepdims=True)
        m_ref[g] = m_new
        p_refs[slot][g] = p.astype(BF16)
        a_refs[slot][g] = alpha

    def step(j, slot):
        for g in range(heads):
            scores(g, j + 1, 1 - slot)
        for g in range(heads):
            weighted_values(g, jnp.maximum(j - 1, 0), 1 - slot)
        for g in range(heads):
            softmax(g, slot, False)

    def last_step(slot):
        for g in range(heads):
            weighted_values(g, jnp.maximum(qi - 1, 0), 1 - slot)
        for g in range(heads):
            softmax(g, slot, True)
        for g in range(heads):
            weighted_values(g, qi, slot)

    for g in range(heads):
        scores(g, 0, 0)

    def pair(t, carry):
        step(2 * t, 0)
        step(2 * t + 1, 1)
        return carry

    lax.fori_loop(0, qi // 2, pair, 0)
    odd = qi % 2 == 1

    @pl.when(odd)
    def _():
        step(qi - 1, 0)
        last_step(1)

    @pl.when(jnp.logical_not(odd))
    def _():
        last_step(0)

    for g in range(heads):
        o_ref[:, g * HEAD_DIM:(g + 1) * HEAD_DIM] = (acc_ref[g] / l_ref[g]).T.astype(o_ref.dtype)


def _forgetting_attention(q_t, qc, k, kc, v_t, *, tq, heads=2):
    h, _, s = q_t.shape
    nk = s // tq
    q_spec = pl.BlockSpec((heads, HEAD_DIM, tq), lambda hg, qi: (hg, 0, qi))
    k_spec = pl.BlockSpec((heads, nk, tq, HEAD_DIM), lambda hg, qi: (hg, 0, 0, 0))
    return pl.pallas_call(
        functools.partial(_attn_kernel, tq=tq, heads=heads),
        grid=(h // heads, nk),
        in_specs=[q_spec, q_spec, k_spec, k_spec,
                  pl.BlockSpec((heads, nk, HEAD_DIM, tq), lambda hg, qi: (hg, 0, 0, 0))],
        out_specs=pl.BlockSpec((tq, heads * HEAD_DIM), lambda hg, qi: (qi, hg)),
        out_shape=jax.ShapeDtypeStruct((s, h * HEAD_DIM), BF16),
        scratch_shapes=[pltpu.VMEM((heads, MXU_DEPTH, tq), BF16),
                        pltpu.VMEM((heads, nk, tq, MXU_DEPTH), BF16),
                        pltpu.VMEM((heads, 1, tq), F32), pltpu.VMEM((heads, 1, tq), F32),
                        pltpu.VMEM((heads, HEAD_DIM, tq), F32),
                        pltpu.VMEM((heads, tq, tq), F32), pltpu.VMEM((heads, tq, tq), F32),
                        pltpu.VMEM((heads, tq, tq), BF16), pltpu.VMEM((heads, tq, tq), BF16),
                        pltpu.VMEM((heads, 1, tq), F32), pltpu.VMEM((heads, 1, tq), F32)],
        compiler_params=_params("parallel", "arbitrary"),
        name="forgetting_attention",
    )(q_t, qc, k, kc, v_t)


def _merge_kernel(x_ref, ac_ref, at_ref, gc_ref, ga_ref, wc_ref, wa_ref, wo_ref, o_ref):
    yc = jnp.dot(ac_ref[...], wc_ref[...], preferred_element_type=F32)
    ya = jnp.dot(at_ref[...], wa_ref[...], preferred_element_type=F32)
    merged = gc_ref[...].astype(F32) * yc + ga_ref[...].astype(F32) * ya
    o_ref[...] = x_ref[...] + jnp.dot(merged.astype(BF16), wo_ref[...], preferred_element_type=F32)


def _merge(x, a_conv, a_attn, gates, w_conv_out, w_attn_out, w_out, *, layer, tm=256):
    s, d = x.shape
    const = lambda i: (layer, 0, 0)
    return pl.pallas_call(
        _merge_kernel,
        grid=(s // tm,),
        in_specs=[pl.BlockSpec((tm, d), lambda i: (i, 0)),
                  pl.BlockSpec((tm, CONV_CH), lambda i: (i, 0)),
                  pl.BlockSpec((tm, ATTN_W), lambda i: (i, 0)),
                  pl.BlockSpec((tm, d), lambda i: (i, 0)),
                  pl.BlockSpec((tm, d), lambda i: (i, 1)),
                  pl.BlockSpec((None, CONV_CH, d), const),
                  pl.BlockSpec((None, ATTN_W, d), const),
                  pl.BlockSpec((None, d, d), const)],
        out_specs=pl.BlockSpec((tm, d), lambda i: (i, 0)),
        out_shape=jax.ShapeDtypeStruct((s, d), F32),
        compiler_params=_params("parallel"),
        name="gated_merge",
    )(x, a_conv, a_attn, gates, gates, w_conv_out, w_attn_out, w_out)


def _ffn_kernel(x_ref, g_ref, wg_ref, wu_ref, wd_ref, o_ref, h_ref):
    @pl.when(pl.program_id(1) == 0)
    def _():
        x = x_ref[...]
        h_ref[...] = _rms(x, g_ref[...]).astype(BF16)
        o_ref[...] = x

    h = h_ref[...]
    gate = jnp.dot(h, wg_ref[...].astype(BF16), preferred_element_type=F32)
    up = jnp.dot(h, wu_ref[...].astype(BF16), preferred_element_type=F32)
    a = (gate * jax.nn.sigmoid(gate) * up).astype(BF16)
    o_ref[...] += jnp.dot(a, wd_ref[...].astype(BF16), preferred_element_type=F32)


def _ffn(x, g, w_gate_up, w_down, *, layer, tm=1024, th=256):
    s, d = x.shape
    nh = FFN_HIDDEN // th
    return pl.pallas_call(
        _ffn_kernel,
        grid=(s // tm, nh),
        in_specs=[pl.BlockSpec((tm, d), lambda i, c: (i, 0), pipeline_mode=pl.Buffered(1)),
                  pl.BlockSpec((1, d), lambda i, c: (0, 0)),
                  pl.BlockSpec((None, d, th), lambda i, c: (layer, 0, c)),
                  pl.BlockSpec((None, d, th), lambda i, c: (layer, 0, c + nh)),
                  pl.BlockSpec((None, th, d), lambda i, c: (layer, c, 0))],
        out_specs=pl.BlockSpec((tm, d), lambda i, c: (i, 0)),
        out_shape=jax.ShapeDtypeStruct((s, d), F32),
        scratch_shapes=[pltpu.VMEM((tm, d), BF16)],
        compiler_params=_params("parallel", "arbitrary"),
        name="swiglu_ffn",
    )(x, g, w_gate_up, w_gate_up, w_down)


def _ple_kernel(x_ref, g_ref, p_ref, wg_ref, wp_ref, fg_ref, o_ref, *, final):
    x = x_ref[...]
    h = _rms(x, g_ref[...]).astype(BF16)
    gate = jax.nn.sigmoid(jnp.dot(h, wg_ref[...], preferred_element_type=F32))
    emb = jnp.dot(p_ref[...].astype(BF16), wp_ref[...], preferred_element_type=F32)
    y = x + gate * emb
    if final:
        y = _rms(y, fg_ref[...])
    o_ref[...] = y


def _ple(x, g, p, w_gate, w_proj, final_g, *, layer, final, tm=256):
    s, d = x.shape
    pd = p.shape[-1]
    const = lambda i: (0, 0)
    stacked = lambda i: (layer, 0, 0)
    return pl.pallas_call(
        functools.partial(_ple_kernel, final=final),
        grid=(s // tm,),
        in_specs=[pl.BlockSpec((tm, d), lambda i: (i, 0)),
                  pl.BlockSpec((1, d), const),
                  pl.BlockSpec((None, tm, pd), lambda i: (layer, i, 0)),
                  pl.BlockSpec((None, d, d), stacked),
                  pl.BlockSpec((None, pd, d), stacked),
                  pl.BlockSpec((1, d), const)],
        out_specs=pl.BlockSpec((tm, d), lambda i: (i, 0)),
        out_shape=jax.ShapeDtypeStruct((s, d), F32),
        compiler_params=_params("parallel"),
        name="ple_final" if final else "ple",
    )(x, g, p, w_gate, w_proj, final_g)


def kernel(x, p, norm_mix_g, w_in, b_forget, conv_w, conv_b, conv_ln_g, conv_ln_b, w_conv_out,
           w_attn_out, w_out, norm_ffn_g, w_gate_up, w_down, norm_ple_g, w_ple_gate, w_ple_proj, final_g):
    b, s, d = x.shape
    depth = w_in.shape[0]
    assert b == 1 and d == D_MODEL
    xs = x.reshape(s, d)
    row = lambda v: v.reshape(1, -1)
    o_f = 2 * CONV_CH + 3 * ATTN_W
    o_g = o_f + N_HEADS
    tq = 512
    w_conv_out, w_attn_out, w_out, w_ple_gate, w_ple_proj = (
        w.astype(BF16) for w in (w_conv_out, w_attn_out, w_out, w_ple_gate, w_ple_proj))
    p = p.reshape(depth, s, -1)

    for i in range(depth):
        wi = w_in[i]
        w_perm = jnp.concatenate([wi[:, :o_f], wi[:, o_g:], wi[:, o_f:o_g],
                                  jnp.zeros((d, LANES - N_HEADS), F32)], axis=1).astype(BF16)
        b_f = jnp.pad(b_forget[i], (0, LANES - N_HEADS)).reshape(1, LANES)
        g_mix = row(norm_mix_g[i])
        c_glu, c_qkv, c_gates, c_f = 0, 2 * CONV_CH, o_f, o_f + 2 * D_MODEL

        u = _norm_matmul(xs, g_mix, w_perm, [c_glu // 512, (c_glu + CONV_CH) // 512], [], _ep_glu, F32,
                         n=CONV_CH, tm=1024, tn=512, name="in_proj_glu")
        q_t, k, v_t = _qkv_proj(xs, g_mix, w_perm, c_qkv // ATTN_W, tm=1024, tq=tq)
        gates = _norm_matmul(xs, g_mix, w_perm, [c_gates // 1024], [], _ep_sigmoid, BF16,
                             n=2 * D_MODEL, tm=1024, tn=1024, name="in_proj_gates")
        log_f = _norm_matmul(xs, g_mix, w_perm, [c_f // LANES], [b_f], _ep_log_forget, F32,
                             n=LANES, tm=1024, tn=LANES, name="in_proj_forget")

        qc, kc = _cumsum_operands(log_f, tb=tq)
        a_attn = _forgetting_attention(q_t, qc, k, kc, v_t, tq=tq)

        a_conv = _conformer_conv(u, conv_w[i], row(conv_b[i]), row(conv_ln_g[i]), row(conv_ln_b[i]))

        xs = _merge(xs, a_conv, a_attn, gates, w_conv_out, w_attn_out, w_out, layer=i)
        xs = _ffn(xs, row(norm_ffn_g[i]), w_gate_up, w_down, layer=i)
        xs = _ple(xs, row(norm_ple_g[i]), p, w_ple_gate, w_ple_proj, row(final_g), layer=i,
                  final=(i == depth - 1))
    return xs.reshape(b, s, d)
```

```python
import functools
import math

import jax
import jax.numpy as jnp
from jax import lax
from jax.experimental import pallas as pl
from jax.experimental.pallas import tpu as pltpu

D_MODEL = 2048
N_HEADS = 8
HEAD_DIM = 128
ATTN_W = N_HEADS * HEAD_DIM
CONV_CH = D_MODEL // 2
CONV_K = 31
FFN_HIDDEN = 5632
EPS = 1e-6
NEG_INF = -1e30
LOG2E = math.log2(math.e)

LANES = 128
SUBLANES = 8
MXU_DEPTH = 256
CONV_HALO = 32
VMEM_LIMIT_BYTES = 56 * 1024 * 1024

BF16 = jnp.bfloat16
F32 = jnp.float32


def _params(*sem):
    return pltpu.CompilerParams(dimension_semantics=sem, vmem_limit_bytes=VMEM_LIMIT_BYTES)


def _rms(x, g):
    ms = jnp.mean(x * x, axis=-1, keepdims=True)
    return x * lax.rsqrt(ms + EPS) * g


def _split3(x):
    hi = x.astype(BF16)
    r1 = x - hi.astype(F32)
    mid = r1.astype(BF16)
    lo = (r1 - mid.astype(F32)).astype(BF16)
    return hi, mid, lo


def _w_in_prep_kernel(w_ref, o_ref, *, o_f, n_gate):
    o_g = o_f + N_HEADS
    o_ref[:, :o_f] = w_ref[:, :o_f].astype(BF16)
    o_ref[:, o_f:o_f + n_gate] = w_ref[:, o_g:o_g + n_gate].astype(BF16)
    tile = w_ref[:, o_f:o_f + LANES]
    lane = lax.broadcasted_iota(jnp.int32, tile.shape, 1)
    o_ref[:, o_f + n_gate:] = jnp.where(lane < N_HEADS, tile, 0.0).astype(BF16)


def _w_in_prep(w_in, *, tr=128):
    depth, k, n = w_in.shape
    o_f = 2 * CONV_CH + 3 * ATTN_W
    n_gate = 2 * D_MODEL
    n_out = o_f + n_gate + LANES
    return pl.pallas_call(
        functools.partial(_w_in_prep_kernel, o_f=o_f, n_gate=n_gate),
        grid=(depth, k // tr),
        in_specs=[pl.BlockSpec((None, tr, n), lambda l, i: (l, i, 0))],
        out_specs=pl.BlockSpec((None, tr, n_out), lambda l, i: (l, i, 0)),
        out_shape=jax.ShapeDtypeStruct((depth, k, n_out), BF16),
        compiler_params=_params("parallel", "parallel"),
        name="w_in_prep",
    )(w_in)


def _rmsnorm_kernel(x_ref, g_ref, o_ref):
    o_ref[...] = _rms(x_ref[...], g_ref[...]).astype(o_ref.dtype)


def _rmsnorm_cast(x, g, *, tm=512):
    m, k = x.shape
    return pl.pallas_call(
        _rmsnorm_kernel,
        grid=(m // tm,),
        in_specs=[pl.BlockSpec((tm, k), lambda i: (i, 0)), pl.BlockSpec((1, k), lambda i: (0, 0))],
        out_specs=pl.BlockSpec((tm, k), lambda i: (i, 0)),
        out_shape=jax.ShapeDtypeStruct((m, k), BF16),
        compiler_params=_params("parallel"),
        name="mix_norm",
    )(x, g)


def _mm_kernel(*refs, n_w, n_extra, epilogue):
    h_ref = refs[0]
    w_refs = refs[1:1 + n_w]
    e_refs = refs[1 + n_w:1 + n_w + n_extra]
    o_ref = refs[1 + n_w + n_extra]
    h = h_ref[...]
    accs = [jnp.dot(h, w[...], preferred_element_type=F32) for w in w_refs]
    o_ref[...] = epilogue(accs, [e[...] for e in e_refs]).astype(o_ref.dtype)


def _in_proj(h, w, col_tiles, extras, epilogue, out_dtype, *, layer, n, tm, tn, name):
    m, k = h.shape
    grid = (m // tm, n // tn)
    in_specs = [pl.BlockSpec((tm, k), lambda i, j: (i, 0))]
    in_specs += [pl.BlockSpec((None, k, tn), functools.partial(lambda i, j, c: (layer, 0, j + c), c=c))
                 for c in col_tiles]
    in_specs += [pl.BlockSpec((1, tn), lambda i, j: (0, j)) for _ in extras]
    return pl.pallas_call(
        functools.partial(_mm_kernel, n_w=len(col_tiles), n_extra=len(extras), epilogue=epilogue),
        grid=grid,
        in_specs=in_specs,
        out_specs=pl.BlockSpec((tm, tn), lambda i, j: (i, j)),
        out_shape=jax.ShapeDtypeStruct((m, n), out_dtype),
        compiler_params=_params("parallel", "arbitrary"),
        name=name,
    )(h, *([w] * len(col_tiles)), *extras)


def _ep_sigmoid(accs, extras):
    return jax.nn.sigmoid(accs[0])


def _ep_glu(accs, extras):
    return accs[0] * jax.nn.sigmoid(accs[1])


def _ep_log_forget(accs, extras):
    return jax.nn.log_sigmoid(accs[0] + extras[0])


def _qkv_kernel(h_ref, w_ref, qt_ref, k_ref, vt_ref, *, tq, q_scale):
    j = pl.program_id(1)
    acc = jnp.dot(h_ref[...], w_ref[...], preferred_element_type=F32)
    head = lambda h, r: acc[r * tq:(r + 1) * tq, h * HEAD_DIM:(h + 1) * HEAD_DIM]
    halves = k_ref.shape[1]

    @pl.when(j == 0)
    def _():
        for h in range(N_HEADS):
            for r in range(halves):
                qt_ref[h, :, r * tq:(r + 1) * tq] = (head(h, r) * q_scale).T.astype(BF16)

    @pl.when(j == 1)
    def _():
        for h in range(N_HEADS):
            for r in range(halves):
                k_ref[h, r] = head(h, r).astype(BF16)

    @pl.when(j == 2)
    def _():
        for h in range(N_HEADS):
            for r in range(halves):
                vt_ref[h, r] = head(h, r).T.astype(BF16)


def _qkv_proj(h, w, col_tile, *, layer, tm, tq):
    m, k = h.shape
    nk = m // tq
    halves = tm // tq
    q_scale = LOG2E / math.sqrt(HEAD_DIM)
    return pl.pallas_call(
        functools.partial(_qkv_kernel, tq=tq, q_scale=q_scale),
        grid=(m // tm, 3),
        in_specs=[pl.BlockSpec((tm, k), lambda i, j: (i, 0)),
                  pl.BlockSpec((None, k, ATTN_W), lambda i, j: (layer, 0, j + col_tile))],
        out_specs=[pl.BlockSpec((N_HEADS, HEAD_DIM, tm), lambda i, j: (0, 0, i)),
                   pl.BlockSpec((N_HEADS, halves, tq, HEAD_DIM), lambda i, j: (0, i, 0, 0)),
                   pl.BlockSpec((N_HEADS, halves, HEAD_DIM, tq), lambda i, j: (0, i, 0, 0))],
        out_shape=[jax.ShapeDtypeStruct((N_HEADS, HEAD_DIM, m), BF16),
                   jax.ShapeDtypeStruct((N_HEADS, nk, tq, HEAD_DIM), BF16),
                   jax.ShapeDtypeStruct((N_HEADS, nk, HEAD_DIM, tq), BF16)],
        compiler_params=_params("parallel", "arbitrary"),
        name="in_proj_qkv",
    )(h, w)


def _cumsum_kernel(x_ref, qc_ref, kc_ref, carry_ref, *, tb):
    @pl.when(pl.program_id(0) == 0)
    def _():
        carry_ref[...] = jnp.zeros_like(carry_ref)

    row = lax.broadcasted_iota(jnp.int32, (tb, tb), 0)
    col = lax.broadcasted_iota(jnp.int32, (tb, tb), 1)
    tri = jnp.where(col <= row, 1.0, 0.0).astype(BF16)
    s = carry_ref[...]
    for limb in _split3(x_ref[...]):
        s = s + jnp.dot(tri, limb, preferred_element_type=F32)
    carry_ref[...] = s[tb - 1:tb, :]
    c2 = s * LOG2E
    limbs_col = jnp.concatenate(_split3(c2), axis=1)
    limbs_row = [limb.astype(F32) for limb in _split3(c2.T)]

    sel_r = lax.broadcasted_iota(jnp.int32, (3 * LANES, HEAD_DIM), 0)
    sel_c = lax.broadcasted_iota(jnp.int32, (3 * LANES, HEAD_DIM), 1)
    lane = lax.broadcasted_iota(jnp.int32, (tb, HEAD_DIM), 1)
    ones_cols = jnp.where(lane < 3, 1.0, 0.0)
    sub = lax.broadcasted_iota(jnp.int32, (HEAD_DIM, tb), 0)
    ones_rows = jnp.where(sub < 3, 0.0, jnp.where(sub < 6, 1.0, 0.0))
    for h in range(N_HEADS):
        target = jnp.where(sel_c < 3, -1, jnp.where(sel_c < 6, (sel_c - 3) * LANES + h, -1))
        pick = jnp.where(sel_r == target, -1.0, 0.0).astype(BF16)
        kc = jnp.dot(limbs_col, pick, preferred_element_type=F32) + ones_cols
        kc_ref[h, 0] = kc.astype(BF16)
        qc = ones_rows
        for l in range(3):
            qc = jnp.where(sub == l, limbs_row[l][h:h + 1, :], qc)
        qc_ref[h] = qc.astype(BF16)


def _cumsum_operands(x, *, tb):
    m, n = x.shape
    return pl.pallas_call(
        functools.partial(_cumsum_kernel, tb=tb),
        grid=(m // tb,),
        in_specs=[pl.BlockSpec((tb, n), lambda i: (i, 0))],
        out_specs=[pl.BlockSpec((N_HEADS, HEAD_DIM, tb), lambda i: (0, 0, i)),
                   pl.BlockSpec((N_HEADS, 1, tb, HEAD_DIM), lambda i: (0, i, 0, 0))],
        out_shape=[jax.ShapeDtypeStruct((N_HEADS, HEAD_DIM, m), BF16),
                   jax.ShapeDtypeStruct((N_HEADS, m // tb, tb, HEAD_DIM), BF16)],
        scratch_shapes=[pltpu.VMEM((1, n), F32)],
        compiler_params=_params("arbitrary"),
        name="forget_cumsum",
    )(x)


def _conv_kernel(u_ref, halo_ref, w_ref, b_ref, lg_ref, lb_ref, o_ref, buf_ref, sh_ref, y_ref, *, ts, rc, rn):
    i = pl.program_id(0)
    span = ts + CONV_HALO - SUBLANES
    off = CONV_HALO - (CONV_K - 1)

    for lb in range(CONV_CH // LANES):
        lanes = slice(lb * LANES, (lb + 1) * LANES)
        halo = halo_ref[:, lanes]
        buf_ref[lb, 0:CONV_HALO, :] = jnp.where(i == 0, jnp.zeros_like(halo), halo)
        buf_ref[lb, CONV_HALO:, :] = u_ref[:, lanes]
        for r in range(1, SUBLANES):
            sh_ref[r - 1, lb] = buf_ref[lb, r:r + span, :]

    for lb in range(CONV_CH // LANES):
        lanes = slice(lb * LANES, (lb + 1) * LANES)

        def chunk(c, carry, lb=lb, lanes=lanes):
            base = pl.multiple_of(c * rc, rc)
            acc = [jnp.zeros((rc // SUBLANES, SUBLANES, LANES), F32) for _ in range(2)]
            for k in range(CONV_K):
                r = (off + k) % SUBLANES
                a = (off + k) - r
                win = (buf_ref[lb, pl.ds(base + a, rc), :] if r == 0
                       else sh_ref[r - 1, lb, pl.ds(base + a, rc), :])
                acc[k % 2] = acc[k % 2] + win.reshape(rc // SUBLANES, SUBLANES, LANES) * w_ref[k, :, lanes][None]
            y_ref[pl.ds(base, rc), lanes] = (acc[0] + acc[1]).reshape(rc, LANES)
            return carry

        lax.fori_loop(0, ts // rc, chunk, 0)

    def norm(c, carry):
        base = pl.multiple_of(c * rn, rn)
        acc = y_ref[pl.ds(base, rn), :] + b_ref[...]
        mu = jnp.mean(acc, axis=-1, keepdims=True)
        d = acc - mu
        var = jnp.mean(d * d, axis=-1, keepdims=True)
        y = d * lax.rsqrt(var + EPS) * lg_ref[...] + lb_ref[...]
        o_ref[pl.ds(base, rn), :] = (y * jax.nn.sigmoid(y)).astype(o_ref.dtype)
        return carry

    lax.fori_loop(0, ts // rn, norm, 0)


def _conformer_conv(u, conv_w, conv_b, ln_g, ln_b, *, ts=512, rc=128, rn=128):
    s, c = u.shape
    halo_blocks = ts // CONV_HALO
    conv_w = jnp.broadcast_to(conv_w[:, None, :], (CONV_K, SUBLANES, c))
    return pl.pallas_call(
        functools.partial(_conv_kernel, ts=ts, rc=rc, rn=rn),
        grid=(s // ts,),
        in_specs=[pl.BlockSpec((ts, c), lambda i: (i, 0)),
                  pl.BlockSpec((CONV_HALO, c), lambda i: (jnp.maximum(i * halo_blocks - 1, 0), 0)),
                  pl.BlockSpec((CONV_K, SUBLANES, c), lambda i: (0, 0, 0)),
                  pl.BlockSpec((1, c), lambda i: (0, 0)),
                  pl.BlockSpec((1, c), lambda i: (0, 0)),
                  pl.BlockSpec((1, c), lambda i: (0, 0))],
        out_specs=pl.BlockSpec((ts, c), lambda i: (i, 0)),
        out_shape=jax.ShapeDtypeStruct((s, c), BF16),
        scratch_shapes=[pltpu.VMEM((c // LANES, ts + CONV_HALO, LANES), F32),
                        pltpu.VMEM((SUBLANES - 1, c // LANES, ts + CONV_HALO - SUBLANES, LANES), F32),
                        pltpu.VMEM((ts, c), F32)],
        compiler_params=_params("parallel"),
        name="conformer_conv",
    )(u, u, conv_w, conv_b, ln_g, ln_b)


def _attn_kernel(qt_ref, qc_ref, k_ref, kc_ref, vt_ref, o_ref, qa_ref, ka_ref, m_ref, l_ref, acc_ref,
                 s0_ref, s1_ref, p0_ref, p1_ref, a0_ref, a1_ref, *, tq, heads):
    qi = pl.program_id(1)
    nk = k_ref.shape[1]
    s_refs, p_refs, a_refs = (s0_ref, s1_ref), (p0_ref, p1_ref), (a0_ref, a1_ref)

    @pl.when(qi == 0)
    def _():
        def widen(j, carry):
            for g in range(heads):
                ka_ref[g, j, :, :HEAD_DIM] = k_ref[g, j]
                ka_ref[g, j, :, HEAD_DIM:] = kc_ref[g, j]
            return carry
        lax.fori_loop(0, nk, widen, 0)

    qa_ref[:, :HEAD_DIM, :] = qt_ref[...]
    qa_ref[:, HEAD_DIM:, :] = qc_ref[...]
    m_ref[...] = jnp.full_like(m_ref, NEG_INF)
    l_ref[...] = jnp.zeros_like(l_ref)
    acc_ref[...] = jnp.zeros_like(acc_ref)
    for slot in range(2):
        p_refs[slot][...] = jnp.zeros_like(p_refs[slot])
        a_refs[slot][...] = jnp.ones_like(a_refs[slot])

    def scores(g, j, slot):
        s_refs[slot][g] = jnp.dot(ka_ref[g, j], qa_ref[g], preferred_element_type=F32)

    def weighted_values(g, j, slot):
        acc_ref[g] = a_refs[slot][g] * acc_ref[g] + jnp.dot(vt_ref[g, j], p_refs[slot][g],
                                                            preferred_element_type=F32)

    def softmax(g, slot, diagonal):
        s = s_refs[slot][g]
        if diagonal:
            key = lax.broadcasted_iota(jnp.int32, (tq, tq), 0)
            qry = lax.broadcasted_iota(jnp.int32, (tq, tq), 1)
            s = jnp.where(key <= qry, s, NEG_INF)
        m_prev = m_ref[g]
        m_new = jnp.maximum(m_prev, jnp.max(s, axis=0, keepdims=True))
        alpha = jnp.exp2(m_prev - m_new)
        p = jnp.exp2(s - m_new)
        l_ref[g] = alpha * l_ref[g] + jnp.sum(p, axis=0, keepdims=True)
        m_ref[g] = m_new
        p_refs[slot][g] = p.astype(BF16)
        a_refs[slot][g] = alpha

    def step(j, slot):
        for g in range(heads):
            scores(g, j + 1, 1 - slot)
        for g in range(heads):
            weighted_values(g, jnp.maximum(j - 1, 0), 1 - slot)
        for g in range(heads):
            softmax(g, slot, False)

    def last_step(slot):
        for g in range(heads):
            weighted_values(g, jnp.maximum(qi - 1, 0), 1 - slot)
        for g in range(heads):
            softmax(g, slot, True)
        for g in range(heads):
            weighted_values(g, qi, slot)

    for g in range(heads):
        scores(g, 0, 0)

    def pair(t, carry):
        step(2 * t, 0)
        step(2 * t + 1, 1)
        return carry

    lax.fori_loop(0, qi // 2, pair, 0)
    odd = qi % 2 == 1

    @pl.when(odd)
    def _():
        step(qi - 1, 0)
        last_step(1)

    @pl.when(jnp.logical_not(odd))
    def _():
        last_step(0)

    for g in range(heads):
        o_ref[:, g * HEAD_DIM:(g + 1) * HEAD_DIM] = (acc_ref[g] / l_ref[g]).T.astype(o_ref.dtype)


def _forgetting_attention(q_t, qc, k, kc, v_t, *, tq, heads=2):
    h, _, s = q_t.shape
    nk = s // tq
    q_spec = pl.BlockSpec((heads, HEAD_DIM, tq), lambda hg, qi: (hg, 0, qi))
    k_spec = pl.BlockSpec((heads, nk, tq, HEAD_DIM), lambda hg, qi: (hg, 0, 0, 0))
    return pl.pallas_call(
        functools.partial(_attn_kernel, tq=tq, heads=heads),
        grid=(h // heads, nk),
        in_specs=[q_spec, q_spec, k_spec, k_spec,
                  pl.BlockSpec((heads, nk, HEAD_DIM, tq), lambda hg, qi: (hg, 0, 0, 0))],
        out_specs=pl.BlockSpec((tq, heads * HEAD_DIM), lambda hg, qi: (qi, hg)),
        out_shape=jax.ShapeDtypeStruct((s, h * HEAD_DIM), BF16),
        scratch_shapes=[pltpu.VMEM((heads, MXU_DEPTH, tq), BF16),
                        pltpu.VMEM((heads, nk, tq, MXU_DEPTH), BF16),
                        pltpu.VMEM((heads, 1, tq), F32), pltpu.VMEM((heads, 1, tq), F32),
                        pltpu.VMEM((heads, HEAD_DIM, tq), F32),
                        pltpu.VMEM((heads, tq, tq), F32), pltpu.VMEM((heads, tq, tq), F32),
                        pltpu.VMEM((heads, tq, tq), BF16), pltpu.VMEM((heads, tq, tq), BF16),
                        pltpu.VMEM((heads, 1, tq), F32), pltpu.VMEM((heads, 1, tq), F32)],
        compiler_params=_params("parallel", "arbitrary"),
        name="forgetting_attention",
    )(q_t, qc, k, kc, v_t)


def _merge_kernel(x_ref, ac_ref, at_ref, gc_ref, ga_ref, wc_ref, wa_ref, wo_ref, o_ref):
    yc = jnp.dot(ac_ref[...], wc_ref[...], preferred_element_type=F32)
    ya = jnp.dot(at_ref[...], wa_ref[...], preferred_element_type=F32)
    merged = gc_ref[...].astype(F32) * yc + ga_ref[...].astype(F32) * ya
    o_ref[...] = x_ref[...] + jnp.dot(merged.astype(BF16), wo_ref[...], preferred_element_type=F32)


def _merge(x, a_conv, a_attn, gates, w_conv_out, w_attn_out, w_out, *, layer, tm=256):
    s, d = x.shape
    const = lambda i: (layer, 0, 0)
    return pl.pallas_call(
        _merge_kernel,
        grid=(s // tm,),
        in_specs=[pl.BlockSpec((tm, d), lambda i: (i, 0)),
                  pl.BlockSpec((tm, CONV_CH), lambda i: (i, 0)),
                  pl.BlockSpec((tm, ATTN_W), lambda i: (i, 0)),
                  pl.BlockSpec((tm, d), lambda i: (i, 0)),
                  pl.BlockSpec((tm, d), lambda i: (i, 1)),
                  pl.BlockSpec((None, CONV_CH, d), const),
                  pl.BlockSpec((None, ATTN_W, d), const),
                  pl.BlockSpec((None, d, d), const)],
        out_specs=pl.BlockSpec((tm, d), lambda i: (i, 0)),
        out_shape=jax.ShapeDtypeStruct((s, d), F32),
        compiler_params=_params("parallel"),
        name="gated_merge",
    )(x, a_conv, a_attn, gates, gates, w_conv_out, w_attn_out, w_out)


def _ffn_kernel(x_ref, g_ref, wg_ref, wu_ref, wd_ref, o_ref, h_ref):
    @pl.when(pl.program_id(1) == 0)
    def _():
        x = x_ref[...]
        h_ref[...] = _rms(x, g_ref[...]).astype(BF16)
        o_ref[...] = x

    h = h_ref[...]
    gate = jnp.dot(h, wg_ref[...].astype(BF16), preferred_element_type=F32)
    up = jnp.dot(h, wu_ref[...].astype(BF16), preferred_element_type=F32)
    a = (gate * jax.nn.sigmoid(gate) * up).astype(BF16)
    o_ref[...] += jnp.dot(a, wd_ref[...].astype(BF16), preferred_element_type=F32)


def _ffn(x, g, w_gate_up, w_down, *, layer, tm=1024, th=256):
    s, d = x.shape
    nh = FFN_HIDDEN // th
    return pl.pallas_call(
        _ffn_kernel,
        grid=(s // tm, nh),
        in_specs=[pl.BlockSpec((tm, d), lambda i, c: (i, 0), pipeline_mode=pl.Buffered(1)),
                  pl.BlockSpec((1, d), lambda i, c: (0, 0)),
                  pl.BlockSpec((None, d, th), lambda i, c: (layer, 0, c)),
                  pl.BlockSpec((None, d, th), lambda i, c: (layer, 0, c + nh)),
                  pl.BlockSpec((None, th, d), lambda i, c: (layer, c, 0))],
        out_specs=pl.BlockSpec((tm, d), lambda i, c: (i, 0)),
        out_shape=jax.ShapeDtypeStruct((s, d), F32),
        scratch_shapes=[pltpu.VMEM((tm, d), BF16)],
        compiler_params=_params("parallel", "arbitrary"),
        name="swiglu_ffn",
    )(x, g, w_gate_up, w_gate_up, w_down)


def _ple_kernel(x_ref, g_ref, p_ref, wg_ref, wp_ref, ng_ref, *o_refs, final):
    x = x_ref[...]
    h = _rms(x, g_ref[...]).astype(BF16)
    gate = jax.nn.sigmoid(jnp.dot(h, wg_ref[...], preferred_element_type=F32))
    emb = jnp.dot(p_ref[...].astype(BF16), wp_ref[...], preferred_element_type=F32)
    y = x + gate * emb
    if final:
        o_refs[0][...] = _rms(y, ng_ref[...])
    else:
        o_refs[0][...] = y
        o_refs[1][...] = _rms(y, ng_ref[...]).astype(BF16)


def _ple(x, g, p, w_gate, w_proj, next_g, *, layer, final, tm=256):
    s, d = x.shape
    pd = p.shape[-1]
    const = lambda i: (0, 0)
    stacked = lambda i: (layer, 0, 0)
    row_spec = pl.BlockSpec((tm, d), lambda i: (i, 0))
    out_specs = [row_spec] if final else [row_spec, row_spec]
    out_shape = [jax.ShapeDtypeStruct((s, d), F32)] + ([] if final else [jax.ShapeDtypeStruct((s, d), BF16)])
    return pl.pallas_call(
        functools.partial(_ple_kernel, final=final),
        grid=(s // tm,),
        in_specs=[pl.BlockSpec((tm, d), lambda i: (i, 0)),
                  pl.BlockSpec((1, d), const),
                  pl.BlockSpec((None, tm, pd), lambda i: (layer, i, 0)),
                  pl.BlockSpec((None, d, d), stacked),
                  pl.BlockSpec((None, pd, d), stacked),
                  pl.BlockSpec((1, d), const)],
        out_specs=out_specs,
        out_shape=out_shape,
        compiler_params=_params("parallel"),
        name="ple_final" if final else "ple",
    )(x, g, p, w_gate, w_proj, next_g)


def kernel(x, p, norm_mix_g, w_in, b_forget, conv_w, conv_b, conv_ln_g, conv_ln_b, w_conv_out,
           w_attn_out, w_out, norm_ffn_g, w_gate_up, w_down, norm_ple_g, w_ple_gate, w_ple_proj, final_g):
    b, s, d = x.shape
    depth = w_in.shape[0]
    assert b == 1 and d == D_MODEL
    xs = x.reshape(s, d)
    row = lambda v: v.reshape(1, -1)
    o_f = 2 * CONV_CH + 3 * ATTN_W
    o_g = o_f + N_HEADS
    tq = 512
    w_conv_out, w_attn_out, w_out, w_ple_gate, w_ple_proj = (
        w.astype(BF16) for w in (w_conv_out, w_attn_out, w_out, w_ple_gate, w_ple_proj))
    p = p.reshape(depth, s, -1)
    w_perm = _w_in_prep(w_in)
    c_glu, c_qkv, c_gates, c_f = 0, 2 * CONV_CH, o_f, o_f + 2 * D_MODEL
    h = _rmsnorm_cast(xs, row(norm_mix_g[0]))

    for i in range(depth):
        b_f = jnp.pad(b_forget[i], (0, LANES - N_HEADS)).reshape(1, LANES)

        u = _in_proj(h, w_perm, [c_glu // 512, (c_glu + CONV_CH) // 512], [], _ep_glu, F32,
                     layer=i, n=CONV_CH, tm=1024, tn=512, name="in_proj_glu")
        q_t, k, v_t = _qkv_proj(h, w_perm, c_qkv // ATTN_W, layer=i, tm=1024, tq=tq)
        gates = _in_proj(h, w_perm, [c_gates // 1024], [], _ep_sigmoid, BF16,
                         layer=i, n=2 * D_MODEL, tm=1024, tn=1024, name="in_proj_gates")
        log_f = _in_proj(h, w_perm, [c_f // LANES], [b_f], _ep_log_forget, F32,
                         layer=i, n=LANES, tm=1024, tn=LANES, name="in_proj_forget")

        qc, kc = _cumsum_operands(log_f, tb=tq)
        a_attn = _forgetting_attention(q_t, qc, k, kc, v_t, tq=tq)

        a_conv = _conformer_conv(u, conv_w[i], row(conv_b[i]), row(conv_ln_g[i]), row(conv_ln_b[i]))

        xs = _merge(xs, a_conv, a_attn, gates, w_conv_out, w_attn_out, w_out, layer=i)
        xs = _ffn(xs, row(norm_ffn_g[i]), w_gate_up, w_down, layer=i)
        if i == depth - 1:
            (xs,) = _ple(xs, row(norm_ple_g[i]), p, w_ple_gate, w_ple_proj, row(final_g), layer=i, final=True)
        else:
            xs, h = _ple(xs, row(norm_ple_g[i]), p, w_ple_gate, w_ple_proj, row(norm_mix_g[i + 1]),
                         layer=i, final=False)
    return xs.reshape(b, s, d)
```

```python
import functools
import math

import jax
import jax.numpy as jnp
from jax import lax
from jax.experimental import pallas as pl
from jax.experimental.pallas import tpu as pltpu

D_MODEL = 2048
N_HEADS = 8
HEAD_DIM = 128
ATTN_W = N_HEADS * HEAD_DIM
CONV_CH = D_MODEL // 2
CONV_K = 31
FFN_HIDDEN = 5632
EPS = 1e-6
NEG_INF = -1e30
LOG2E = math.log2(math.e)

LANES = 128
SUBLANES = 8
MXU_DEPTH = 256
CONV_HALO = 32
VMEM_LIMIT_BYTES = 56 * 1024 * 1024

BF16 = jnp.bfloat16
F32 = jnp.float32


def _params(*sem):
    return pltpu.CompilerParams(dimension_semantics=sem, vmem_limit_bytes=VMEM_LIMIT_BYTES)


def _rms(x, g):
    ms = jnp.mean(x * x, axis=-1, keepdims=True)
    return x * lax.rsqrt(ms + EPS) * g


def _split3(x):
    hi = x.astype(BF16)
    r1 = x - hi.astype(F32)
    mid = r1.astype(BF16)
    lo = (r1 - mid.astype(F32)).astype(BF16)
    return hi, mid, lo


def _rmsnorm_kernel(x_ref, g_ref, o_ref):
    o_ref[...] = _rms(x_ref[...], g_ref[...]).astype(o_ref.dtype)


def _rmsnorm_cast(x, g, *, tm=512):
    m, k = x.shape
    return pl.pallas_call(
        _rmsnorm_kernel,
        grid=(m // tm,),
        in_specs=[pl.BlockSpec((tm, k), lambda i: (i, 0)), pl.BlockSpec((1, k), lambda i: (0, 0))],
        out_specs=pl.BlockSpec((tm, k), lambda i: (i, 0)),
        out_shape=jax.ShapeDtypeStruct((m, k), BF16),
        compiler_params=_params("parallel"),
        name="mix_norm",
    )(x, g)


def _mm_kernel(*refs, n_w, n_extra, epilogue):
    h_ref = refs[0]
    w_refs = refs[1:1 + n_w]
    e_refs = refs[1 + n_w:1 + n_w + n_extra]
    o_ref = refs[1 + n_w + n_extra]
    h = h_ref[...]
    accs = [jnp.dot(h, w[...], preferred_element_type=F32) for w in w_refs]
    o_ref[...] = epilogue(accs, [e[...] for e in e_refs]).astype(o_ref.dtype)


def _in_proj(h, w, col_tiles, extras, epilogue, out_dtype, *, layer, n, tm, tn, name):
    m, k = h.shape
    grid = (m // tm, n // tn)
    in_specs = [pl.BlockSpec((tm, k), lambda i, j: (i, 0))]
    in_specs += [pl.BlockSpec((None, k, tn), functools.partial(lambda i, j, c: (layer, 0, j + c), c=c))
                 for c in col_tiles]
    in_specs += [pl.BlockSpec((1, tn), lambda i, j: (0, j)) for _ in extras]
    return pl.pallas_call(
        functools.partial(_mm_kernel, n_w=len(col_tiles), n_extra=len(extras), epilogue=epilogue),
        grid=grid,
        in_specs=in_specs,
        out_specs=pl.BlockSpec((tm, tn), lambda i, j: (i, j)),
        out_shape=jax.ShapeDtypeStruct((m, n), out_dtype),
        compiler_params=_params("parallel", "arbitrary"),
        name=name,
    )(h, *([w] * len(col_tiles)), *extras)


def _ep_sigmoid(accs, extras):
    return jax.nn.sigmoid(accs[0])


def _ep_glu(accs, extras):
    return accs[0] * jax.nn.sigmoid(accs[1])


def _ep_log_forget(accs, extras):
    return jax.nn.log_sigmoid(accs[0] + extras[0])


def _qkv_kernel(h_ref, w_ref, qt_ref, k_ref, vt_ref, *, tq, q_scale):
    j = pl.program_id(1)
    acc = jnp.dot(h_ref[...], w_ref[...], preferred_element_type=F32)
    head = lambda h, r: acc[r * tq:(r + 1) * tq, h * HEAD_DIM:(h + 1) * HEAD_DIM]
    halves = k_ref.shape[1]

    @pl.when(j == 0)
    def _():
        for h in range(N_HEADS):
            for r in range(halves):
                qt_ref[h, :, r * tq:(r + 1) * tq] = (head(h, r) * q_scale).T.astype(BF16)

    @pl.when(j == 1)
    def _():
        for h in range(N_HEADS):
            for r in range(halves):
                k_ref[h, r] = head(h, r).astype(BF16)

    @pl.when(j == 2)
    def _():
        for h in range(N_HEADS):
            for r in range(halves):
                vt_ref[h, r] = head(h, r).T.astype(BF16)


def _qkv_proj(h, w, col_tile, *, layer, tm, tq):
    m, k = h.shape
    nk = m // tq
    halves = tm // tq
    q_scale = LOG2E / math.sqrt(HEAD_DIM)
    return pl.pallas_call(
        functools.partial(_qkv_kernel, tq=tq, q_scale=q_scale),
        grid=(m // tm, 3),
        in_specs=[pl.BlockSpec((tm, k), lambda i, j: (i, 0)),
                  pl.BlockSpec((None, k, ATTN_W), lambda i, j: (layer, 0, j + col_tile))],
        out_specs=[pl.BlockSpec((N_HEADS, HEAD_DIM, tm), lambda i, j: (0, 0, i)),
                   pl.BlockSpec((N_HEADS, halves, tq, HEAD_DIM), lambda i, j: (0, i, 0, 0)),
                   pl.BlockSpec((N_HEADS, halves, HEAD_DIM, tq), lambda i, j: (0, i, 0, 0))],
        out_shape=[jax.ShapeDtypeStruct((N_HEADS, HEAD_DIM, m), BF16),
                   jax.ShapeDtypeStruct((N_HEADS, nk, tq, HEAD_DIM), BF16),
                   jax.ShapeDtypeStruct((N_HEADS, nk, HEAD_DIM, tq), BF16)],
        compiler_params=_params("parallel", "arbitrary"),
        name="in_proj_qkv",
    )(h, w)


def _cumsum_kernel(x_ref, qc_ref, kc_ref, carry_ref, *, tb):
    @pl.when(pl.program_id(0) == 0)
    def _():
        carry_ref[...] = jnp.zeros_like(carry_ref)

    row = lax.broadcasted_iota(jnp.int32, (tb, tb), 0)
    col = lax.broadcasted_iota(jnp.int32, (tb, tb), 1)
    tri = jnp.where(col <= row, 1.0, 0.0).astype(BF16)
    s = carry_ref[...]
    for limb in _split3(x_ref[...]):
        s = s + jnp.dot(tri, limb, preferred_element_type=F32)
    carry_ref[...] = s[tb - 1:tb, :]
    c2 = s * LOG2E
    limbs_col = jnp.concatenate(_split3(c2), axis=1)
    limbs_row = [limb.astype(F32) for limb in _split3(c2.T)]

    sel_r = lax.broadcasted_iota(jnp.int32, (3 * LANES, HEAD_DIM), 0)
    sel_c = lax.broadcasted_iota(jnp.int32, (3 * LANES, HEAD_DIM), 1)
    lane = lax.broadcasted_iota(jnp.int32, (tb, HEAD_DIM), 1)
    ones_cols = jnp.where(lane < 3, 1.0, 0.0)
    sub = lax.broadcasted_iota(jnp.int32, (HEAD_DIM, tb), 0)
    ones_rows = jnp.where(sub < 3, 0.0, jnp.where(sub < 6, 1.0, 0.0))
    for h in range(N_HEADS):
        target = jnp.where(sel_c < 3, -1, jnp.where(sel_c < 6, (sel_c - 3) * LANES + h, -1))
        pick = jnp.where(sel_r == target, -1.0, 0.0).astype(BF16)
        kc = jnp.dot(limbs_col, pick, preferred_element_type=F32) + ones_cols
        kc_ref[h, 0] = kc.astype(BF16)
        qc = ones_rows
        for l in range(3):
            qc = jnp.where(sub == l, limbs_row[l][h:h + 1, :], qc)
        qc_ref[h] = qc.astype(BF16)


def _cumsum_operands(x, *, tb):
    m, n = x.shape
    return pl.pallas_call(
        functools.partial(_cumsum_kernel, tb=tb),
        grid=(m // tb,),
        in_specs=[pl.BlockSpec((tb, n), lambda i: (i, 0))],
        out_specs=[pl.BlockSpec((N_HEADS, HEAD_DIM, tb), lambda i: (0, 0, i)),
                   pl.BlockSpec((N_HEADS, 1, tb, HEAD_DIM), lambda i: (0, i, 0, 0))],
        out_shape=[jax.ShapeDtypeStruct((N_HEADS, HEAD_DIM, m), BF16),
                   jax.ShapeDtypeStruct((N_HEADS, m // tb, tb, HEAD_DIM), BF16)],
        scratch_shapes=[pltpu.VMEM((1, n), F32)],
        compiler_params=_params("arbitrary"),
        name="forget_cumsum",
    )(x)


def _conv_kernel(u_ref, halo_ref, w_ref, b_ref, lg_ref, lb_ref, o_ref, buf_ref, sh_ref, y_ref, *, ts, rc, rn):
    i = pl.program_id(0)
    span = ts + CONV_HALO - SUBLANES
    off = CONV_HALO - (CONV_K - 1)

    for lb in range(CONV_CH // LANES):
        lanes = slice(lb * LANES, (lb + 1) * LANES)
        halo = halo_ref[:, lanes]
        buf_ref[lb, 0:CONV_HALO, :] = jnp.where(i == 0, jnp.zeros_like(halo), halo)
        buf_ref[lb, CONV_HALO:, :] = u_ref[:, lanes]
        for r in range(1, SUBLANES):
            sh_ref[r - 1, lb] = buf_ref[lb, r:r + span, :]

    for lb in range(CONV_CH // LANES):
        lanes = slice(lb * LANES, (lb + 1) * LANES)

        def chunk(c, carry, lb=lb, lanes=lanes):
            base = pl.multiple_of(c * rc, rc)
            acc = [jnp.zeros((rc // SUBLANES, SUBLANES, LANES), F32) for _ in range(2)]
            for k in range(CONV_K):
                r = (off + k) % SUBLANES
                a = (off + k) - r
                win = (buf_ref[lb, pl.ds(base + a, rc), :] if r == 0
                       else sh_ref[r - 1, lb, pl.ds(base + a, rc), :])
                acc[k % 2] = acc[k % 2] + win.reshape(rc // SUBLANES, SUBLANES, LANES) * w_ref[k, :, lanes][None]
            y_ref[pl.ds(base, rc), lanes] = (acc[0] + acc[1]).reshape(rc, LANES)
            return carry

        lax.fori_loop(0, ts // rc, chunk, 0)

    def norm(c, carry):
        base = pl.multiple_of(c * rn, rn)
        acc = y_ref[pl.ds(base, rn), :] + b_ref[...]
        mu = jnp.mean(acc, axis=-1, keepdims=True)
        d = acc - mu
        var = jnp.mean(d * d, axis=-1, keepdims=True)
        y = d * lax.rsqrt(var + EPS) * lg_ref[...] + lb_ref[...]
        o_ref[pl.ds(base, rn), :] = (y * jax.nn.sigmoid(y)).astype(o_ref.dtype)
        return carry

    lax.fori_loop(0, ts // rn, norm, 0)


def _conformer_conv(u, conv_w, conv_b, ln_g, ln_b, *, ts=512, rc=128, rn=128):
    s, c = u.shape
    halo_blocks = ts // CONV_HALO
    conv_w = jnp.broadcast_to(conv_w[:, None, :], (CONV_K, SUBLANES, c))
    return pl.pallas_call(
        functools.partial(_conv_kernel, ts=ts, rc=rc, rn=rn),
        grid=(s // ts,),
        in_specs=[pl.BlockSpec((ts, c), lambda i: (i, 0)),
                  pl.BlockSpec((CONV_HALO, c), lambda i: (jnp.maximum(i * halo_blocks - 1, 0), 0)),
                  pl.BlockSpec((CONV_K, SUBLANES, c), lambda i: (0, 0, 0)),
                  pl.BlockSpec((1, c), lambda i: (0, 0)),
                  pl.BlockSpec((1, c), lambda i: (0, 0)),
                  pl.BlockSpec((1, c), lambda i: (0, 0))],
        out_specs=pl.BlockSpec((ts, c), lambda i: (i, 0)),
        out_shape=jax.ShapeDtypeStruct((s, c), BF16),
        scratch_shapes=[pltpu.VMEM((c // LANES, ts + CONV_HALO, LANES), F32),
                        pltpu.VMEM((SUBLANES - 1, c // LANES, ts + CONV_HALO - SUBLANES, LANES), F32),
                        pltpu.VMEM((ts, c), F32)],
        compiler_params=_params("parallel"),
        name="conformer_conv",
    )(u, u, conv_w, conv_b, ln_g, ln_b)


def _attn_kernel(qt_ref, qc_ref, k_ref, kc_ref, vt_ref, o_ref, qa_ref, ka_ref, m_ref, l_ref, acc_ref,
                 s0_ref, s1_ref, p0_ref, p1_ref, a0_ref, a1_ref, *, tq, heads):
    qi = pl.program_id(1)
    nk = k_ref.shape[1]
    s_refs, p_refs, a_refs = (s0_ref, s1_ref), (p0_ref, p1_ref), (a0_ref, a1_ref)

    @pl.when(qi == 0)
    def _():
        def widen(j, carry):
            for g in range(heads):
                ka_ref[g, j, :, :HEAD_DIM] = k_ref[g, j]
                ka_ref[g, j, :, HEAD_DIM:] = kc_ref[g, j]
            return carry
        lax.fori_loop(0, nk, widen, 0)

    qa_ref[:, :HEAD_DIM, :] = qt_ref[...]
    qa_ref[:, HEAD_DIM:, :] = qc_ref[...]
    m_ref[...] = jnp.full_like(m_ref, NEG_INF)
    l_ref[...] = jnp.zeros_like(l_ref)
    acc_ref[...] = jnp.zeros_like(acc_ref)
    for slot in range(2):
        p_refs[slot][...] = jnp.zeros_like(p_refs[slot])
        a_refs[slot][...] = jnp.ones_like(a_refs[slot])

    def scores(g, j, slot):
        s_refs[slot][g] = jnp.dot(ka_ref[g, j], qa_ref[g], preferred_element_type=F32)

    def weighted_values(g, j, slot):
        acc_ref[g] = a_refs[slot][g] * acc_ref[g] + jnp.dot(vt_ref[g, j], p_refs[slot][g],
                                                            preferred_element_type=F32)

    def softmax(g, slot, diagonal):
        s = s_refs[slot][g]
        if diagonal:
            key = lax.broadcasted_iota(jnp.int32, (tq, tq), 0)
            qry = lax.broadcasted_iota(jnp.int32, (tq, tq), 1)
            s = jnp.where(key <= qry, s, NEG_INF)
        m_prev = m_ref[g]
        m_new = jnp.maximum(m_prev, jnp.max(s, axis=0, keepdims=True))
        alpha = jnp.exp2(m_prev - m_new)
        p = jnp.exp2(s - m_new)
        l_ref[g] = alpha * l_ref[g] + jnp.sum(p, axis=0, keepdims=True)
        m_ref[g] = m_new
        p_refs[slot][g] = p.astype(BF16)
        a_refs[slot][g] = alpha

    def step(j, slot):
        for g in range(heads):
            scores(g, j + 1, 1 - slot)
        for g in range(heads):
            weighted_values(g, jnp.maximum(j - 1, 0), 1 - slot)
        for g in range(heads):
            softmax(g, slot, False)

    def last_step(slot):
        for g in range(heads):
            weighted_values(g, jnp.maximum(qi - 1, 0), 1 - slot)
        for g in range(heads):
            softmax(g, slot, True)
        for g in range(heads):
            weighted_values(g, qi, slot)

    for g in range(heads):
        scores(g, 0, 0)

    def pair(t, carry):
        step(2 * t, 0)
        step(2 * t + 1, 1)
        return carry

    lax.fori_loop(0, qi // 2, pair, 0)
    odd = qi % 2 == 1

    @pl.when(odd)
    def _():
        step(qi - 1, 0)
        last_step(1)

    @pl.when(jnp.logical_not(odd))
    def _():
        last_step(0)

    for g in range(heads):
        o_ref[:, g * HEAD_DIM:(g + 1) * HEAD_DIM] = (acc_ref[g] / l_ref[g]).T.astype(o_ref.dtype)


def _forgetting_attention(q_t, qc, k, kc, v_t, *, tq, heads=2):
    h, _, s = q_t.shape
    nk = s // tq
    q_spec = pl.BlockSpec((heads, HEAD_DIM, tq), lambda hg, qi: (hg, 0, qi))
    k_spec = pl.BlockSpec((heads, nk, tq, HEAD_DIM), lambda hg, qi: (hg, 0, 0, 0))
    return pl.pallas_call(
        functools.partial(_attn_kernel, tq=tq, heads=heads),
        grid=(h // heads, nk),
        in_specs=[q_spec, q_spec, k_spec, k_spec,
                  pl.BlockSpec((heads, nk, HEAD_DIM, tq), lambda hg, qi: (hg, 0, 0, 0))],
        out_specs=pl.BlockSpec((tq, heads * HEAD_DIM), lambda hg, qi: (qi, hg)),
        out_shape=jax.ShapeDtypeStruct((s, h * HEAD_DIM), BF16),
        scratch_shapes=[pltpu.VMEM((heads, MXU_DEPTH, tq), BF16),
                        pltpu.VMEM((heads, nk, tq, MXU_DEPTH), BF16),
                        pltpu.VMEM((heads, 1, tq), F32), pltpu.VMEM((heads, 1, tq), F32),
                        pltpu.VMEM((heads, HEAD_DIM, tq), F32),
                        pltpu.VMEM((heads, tq, tq), F32), pltpu.VMEM((heads, tq, tq), F32),
                        pltpu.VMEM((heads, tq, tq), BF16), pltpu.VMEM((heads, tq, tq), BF16),
                        pltpu.VMEM((heads, 1, tq), F32), pltpu.VMEM((heads, 1, tq), F32)],
        compiler_params=_params("parallel", "arbitrary"),
        name="forgetting_attention",
    )(q_t, qc, k, kc, v_t)


def _merge_kernel(x_ref, ac_ref, at_ref, gc_ref, ga_ref, wc_ref, wa_ref, wo_ref, ng_ref, o_ref, h_ref):
    yc = jnp.dot(ac_ref[...], wc_ref[...], preferred_element_type=F32)
    ya = jnp.dot(at_ref[...], wa_ref[...], preferred_element_type=F32)
    merged = gc_ref[...].astype(F32) * yc + ga_ref[...].astype(F32) * ya
    y = x_ref[...] + jnp.dot(merged.astype(BF16), wo_ref[...], preferred_element_type=F32)
    o_ref[...] = y
    h_ref[...] = _rms(y, ng_ref[...]).astype(BF16)


def _merge(x, a_conv, a_attn, gates, w_conv_out, w_attn_out, w_out, ffn_g, *, layer, tm=512):
    s, d = x.shape
    const = lambda i: (layer, 0, 0)
    once = pl.Buffered(1)
    row_spec = pl.BlockSpec((tm, d), lambda i: (i, 0))
    return pl.pallas_call(
        _merge_kernel,
        grid=(s // tm,),
        in_specs=[row_spec,
                  pl.BlockSpec((tm, CONV_CH), lambda i: (i, 0)),
                  pl.BlockSpec((tm, ATTN_W), lambda i: (i, 0)),
                  pl.BlockSpec((tm, d), lambda i: (i, 0)),
                  pl.BlockSpec((tm, d), lambda i: (i, 1)),
                  pl.BlockSpec((None, CONV_CH, d), const, pipeline_mode=once),
                  pl.BlockSpec((None, ATTN_W, d), const, pipeline_mode=once),
                  pl.BlockSpec((None, d, d), const, pipeline_mode=once),
                  pl.BlockSpec((1, d), lambda i: (0, 0))],
        out_specs=[row_spec, row_spec],
        out_shape=[jax.ShapeDtypeStruct((s, d), F32), jax.ShapeDtypeStruct((s, d), BF16)],
        compiler_params=_params("parallel"),
        name="gated_merge",
    )(x, a_conv, a_attn, gates, gates, w_conv_out, w_attn_out, w_out, ffn_g)


def _ffn_kernel(h_ref, wg_ref, wu_ref, wd_ref, o_ref):
    @pl.when(pl.program_id(1) == 0)
    def _():
        o_ref[...] = jnp.zeros_like(o_ref)

    h = h_ref[...]
    gate = jnp.dot(h, wg_ref[...].astype(BF16), preferred_element_type=F32)
    up = jnp.dot(h, wu_ref[...].astype(BF16), preferred_element_type=F32)
    a = (gate * jax.nn.sigmoid(gate) * up).astype(BF16)
    o_ref[...] += jnp.dot(a, wd_ref[...].astype(BF16), preferred_element_type=F32)


def _ffn(h, w_gate_up, w_down, *, layer, tm=1024, th=256):
    s, d = h.shape
    nh = FFN_HIDDEN // th
    return pl.pallas_call(
        _ffn_kernel,
        grid=(s // tm, nh),
        in_specs=[pl.BlockSpec((tm, d), lambda i, c: (i, 0)),
                  pl.BlockSpec((None, d, th), lambda i, c: (layer, 0, c)),
                  pl.BlockSpec((None, d, th), lambda i, c: (layer, 0, c + nh)),
                  pl.BlockSpec((None, th, d), lambda i, c: (layer, c, 0))],
        out_specs=pl.BlockSpec((tm, d), lambda i, c: (i, 0)),
        out_shape=jax.ShapeDtypeStruct((s, d), F32),
        compiler_params=_params("parallel", "arbitrary"),
        name="swiglu_ffn",
    )(h, w_gate_up, w_gate_up, w_down)


def _ple_kernel(x_ref, f_ref, g_ref, p_ref, wg_ref, wp_ref, ng_ref, *o_refs, final):
    x = x_ref[...] + f_ref[...]
    h = _rms(x, g_ref[...]).astype(BF16)
    gate = jax.nn.sigmoid(jnp.dot(h, wg_ref[...], preferred_element_type=F32))
    emb = jnp.dot(p_ref[...].astype(BF16), wp_ref[...], preferred_element_type=F32)
    y = x + gate * emb
    if final:
        o_refs[0][...] = _rms(y, ng_ref[...])
    else:
        o_refs[0][...] = y
        o_refs[1][...] = _rms(y, ng_ref[...]).astype(BF16)


def _ple(x, f, g, p, w_gate, w_proj, next_g, *, layer, final, tm=256):
    s, d = x.shape
    pd = p.shape[-1]
    const = lambda i: (0, 0)
    stacked = lambda i: (layer, 0, 0)
    once = pl.Buffered(1)
    row_spec = pl.BlockSpec((tm, d), lambda i: (i, 0))
    out_specs = [row_spec] if final else [row_spec, row_spec]
    out_shape = [jax.ShapeDtypeStruct((s, d), F32)] + ([] if final else [jax.ShapeDtypeStruct((s, d), BF16)])
    return pl.pallas_call(
        functools.partial(_ple_kernel, final=final),
        grid=(s // tm,),
        in_specs=[row_spec,
                  row_spec,
                  pl.BlockSpec((1, d), const),
                  pl.BlockSpec((None, tm, pd), lambda i: (layer, i, 0)),
                  pl.BlockSpec((None, d, d), stacked, pipeline_mode=once),
                  pl.BlockSpec((None, pd, d), stacked, pipeline_mode=once),
                  pl.BlockSpec((1, d), const)],
        out_specs=out_specs,
        out_shape=out_shape,
        compiler_params=_params("parallel"),
        name="ple_final" if final else "ple",
    )(x, f, g, p, w_gate, w_proj, next_g)


def kernel(x, p, norm_mix_g, w_in, b_forget, conv_w, conv_b, conv_ln_g, conv_ln_b, w_conv_out,
           w_attn_out, w_out, norm_ffn_g, w_gate_up, w_down, norm_ple_g, w_ple_gate, w_ple_proj, final_g):
    b, s, d = x.shape
    depth = w_in.shape[0]
    assert b == 1 and d == D_MODEL
    xs = x.reshape(s, d)
    row = lambda v: v.reshape(1, -1)
    o_f = 2 * CONV_CH + 3 * ATTN_W
    o_g = o_f + N_HEADS
    tq = 512
    w_conv_out, w_attn_out, w_out, w_ple_gate, w_ple_proj = (
        w.astype(BF16) for w in (w_conv_out, w_attn_out, w_out, w_ple_gate, w_ple_proj))
    p = p.reshape(depth, s, -1)
    w_perm = jnp.concatenate([w_in[:, :, :o_f], w_in[:, :, o_g:], w_in[:, :, o_f:o_g],
                              jnp.zeros((depth, d, LANES - N_HEADS), F32)], axis=2).astype(BF16)
    c_glu, c_qkv, c_gates, c_f = 0, 2 * CONV_CH, o_f, o_f + 2 * D_MODEL
    h = _rmsnorm_cast(xs, row(norm_mix_g[0]))

    for i in range(depth):
        b_f = jnp.pad(b_forget[i], (0, LANES - N_HEADS)).reshape(1, LANES)

        u = _in_proj(h, w_perm, [c_glu // 512, (c_glu + CONV_CH) // 512], [], _ep_glu, F32,
                     layer=i, n=CONV_CH, tm=1024, tn=512, name="in_proj_glu")
        q_t, k, v_t = _qkv_proj(h, w_perm, c_qkv // ATTN_W, layer=i, tm=1024, tq=tq)
        gates = _in_proj(h, w_perm, [c_gates // 1024], [], _ep_sigmoid, BF16,
                         layer=i, n=2 * D_MODEL, tm=1024, tn=1024, name="in_proj_gates")
        log_f = _in_proj(h, w_perm, [c_f // LANES], [b_f], _ep_log_forget, F32,
                         layer=i, n=LANES, tm=1024, tn=LANES, name="in_proj_forget")

        qc, kc = _cumsum_operands(log_f, tb=tq)
        a_attn = _forgetting_attention(q_t, qc, k, kc, v_t, tq=tq)

        a_conv = _conformer_conv(u, conv_w[i], row(conv_b[i]), row(conv_ln_g[i]), row(conv_ln_b[i]))

        xs, h_ffn = _merge(xs, a_conv, a_attn, gates, w_conv_out, w_attn_out, w_out, row(norm_ffn_g[i]), layer=i)
        f = _ffn(h_ffn, w_gate_up, w_down, layer=i)
        if i == depth - 1:
            (xs,) = _ple(xs, f, row(norm_ple_g[i]), p, w_ple_gate, w_ple_proj, row(final_g), layer=i, final=True)
        else:
            xs, h = _ple(xs, f, row(norm_ple_g[i]), p, w_ple_gate, w_ple_proj, row(norm_mix_g[i + 1]),
                         layer=i, final=False)
    return xs.reshape(b, s, d)
```

```python
import functools
import math

import jax
import jax.numpy as jnp
from jax import lax
from jax.experimental import pallas as pl
from jax.experimental.pallas import tpu as pltpu

D_MODEL = 2048
N_HEADS = 8
HEAD_DIM = 128
ATTN_W = N_HEADS * HEAD_DIM
CONV_CH = D_MODEL // 2
CONV_K = 31
FFN_HIDDEN = 5632
EPS = 1e-6
NEG_INF = -1e30
LOG2E = math.log2(math.e)

LANES = 128
SUBLANES = 8
MXU_DEPTH = 256
CONV_HALO = 32
VMEM_LIMIT_BYTES = 56 * 1024 * 1024

BF16 = jnp.bfloat16
F32 = jnp.float32


def _params(*sem):
    return pltpu.CompilerParams(dimension_semantics=sem, vmem_limit_bytes=VMEM_LIMIT_BYTES)


def _rms(x, g):
    ms = jnp.mean(x * x, axis=-1, keepdims=True)
    return x * lax.rsqrt(ms + EPS) * g


def _split3(x):
    hi = x.astype(BF16)
    r1 = x - hi.astype(F32)
    mid = r1.astype(BF16)
    lo = (r1 - mid.astype(F32)).astype(BF16)
    return hi, mid, lo


def _rmsnorm_kernel(x_ref, g_ref, o_ref):
    o_ref[...] = _rms(x_ref[...], g_ref[...]).astype(o_ref.dtype)


def _rmsnorm_cast(x, g, *, tm=512):
    m, k = x.shape
    return pl.pallas_call(
        _rmsnorm_kernel,
        grid=(m // tm,),
        in_specs=[pl.BlockSpec((tm, k), lambda i: (i, 0)), pl.BlockSpec((1, k), lambda i: (0, 0))],
        out_specs=pl.BlockSpec((tm, k), lambda i: (i, 0)),
        out_shape=jax.ShapeDtypeStruct((m, k), BF16),
        compiler_params=_params("parallel"),
        name="mix_norm",
    )(x, g)


def _mm_kernel(*refs, n_w, n_extra, epilogue):
    h_ref = refs[0]
    w_refs = refs[1:1 + n_w]
    e_refs = refs[1 + n_w:1 + n_w + n_extra]
    o_ref = refs[1 + n_w + n_extra]
    h = h_ref[...]
    accs = [jnp.dot(h, w[...], preferred_element_type=F32) for w in w_refs]
    o_ref[...] = epilogue(accs, [e[...] for e in e_refs]).astype(o_ref.dtype)


def _in_proj(h, w, col_tiles, extras, epilogue, out_dtype, *, layer, n, tm, tn, name):
    m, k = h.shape
    grid = (m // tm, n // tn)
    in_specs = [pl.BlockSpec((tm, k), lambda i, j: (i, 0))]
    in_specs += [pl.BlockSpec((None, k, tn), functools.partial(lambda i, j, c: (layer, 0, j + c), c=c))
                 for c in col_tiles]
    in_specs += [pl.BlockSpec((1, tn), lambda i, j: (0, j)) for _ in extras]
    return pl.pallas_call(
        functools.partial(_mm_kernel, n_w=len(col_tiles), n_extra=len(extras), epilogue=epilogue),
        grid=grid,
        in_specs=in_specs,
        out_specs=pl.BlockSpec((tm, tn), lambda i, j: (i, j)),
        out_shape=jax.ShapeDtypeStruct((m, n), out_dtype),
        compiler_params=_params("parallel", "arbitrary"),
        name=name,
    )(h, *([w] * len(col_tiles)), *extras)


def _ep_sigmoid(accs, extras):
    return jax.nn.sigmoid(accs[0])


def _ep_glu(accs, extras):
    return accs[0] * jax.nn.sigmoid(accs[1])


def _ep_log_forget(accs, extras):
    return jax.nn.log_sigmoid(accs[0] + extras[0])


def _qkv_kernel(h_ref, w_ref, qt_ref, k_ref, vt_ref, *, tq, q_scale):
    j = pl.program_id(1)
    acc = jnp.dot(h_ref[...], w_ref[...], preferred_element_type=F32)
    head = lambda h, r: acc[r * tq:(r + 1) * tq, h * HEAD_DIM:(h + 1) * HEAD_DIM]
    halves = k_ref.shape[1]

    @pl.when(j == 0)
    def _():
        for h in range(N_HEADS):
            for r in range(halves):
                qt_ref[h, :, r * tq:(r + 1) * tq] = (head(h, r) * q_scale).T.astype(BF16)

    @pl.when(j == 1)
    def _():
        for h in range(N_HEADS):
            for r in range(halves):
                k_ref[h, r] = head(h, r).astype(BF16)

    @pl.when(j == 2)
    def _():
        for h in range(N_HEADS):
            for r in range(halves):
                vt_ref[h, r] = head(h, r).T.astype(BF16)


def _qkv_proj(h, w, col_tile, *, layer, tm, tq):
    m, k = h.shape
    nk = m // tq
    halves = tm // tq
    q_scale = LOG2E / math.sqrt(HEAD_DIM)
    return pl.pallas_call(
        functools.partial(_qkv_kernel, tq=tq, q_scale=q_scale),
        grid=(m // tm, 3),
        in_specs=[pl.BlockSpec((tm, k), lambda i, j: (i, 0)),
                  pl.BlockSpec((None, k, ATTN_W), lambda i, j: (layer, 0, j + col_tile))],
        out_specs=[pl.BlockSpec((N_HEADS, HEAD_DIM, tm), lambda i, j: (0, 0, i)),
                   pl.BlockSpec((N_HEADS, halves, tq, HEAD_DIM), lambda i, j: (0, i, 0, 0)),
                   pl.BlockSpec((N_HEADS, halves, HEAD_DIM, tq), lambda i, j: (0, i, 0, 0))],
        out_shape=[jax.ShapeDtypeStruct((N_HEADS, HEAD_DIM, m), BF16),
                   jax.ShapeDtypeStruct((N_HEADS, nk, tq, HEAD_DIM), BF16),
                   jax.ShapeDtypeStruct((N_HEADS, nk, HEAD_DIM, tq), BF16)],
        compiler_params=_params("parallel", "arbitrary"),
        name="in_proj_qkv",
    )(h, w)


def _cumsum_kernel(x_ref, qc_ref, kc_ref, carry_ref, *, tb):
    @pl.when(pl.program_id(0) == 0)
    def _():
        carry_ref[...] = jnp.zeros_like(carry_ref)

    row = lax.broadcasted_iota(jnp.int32, (tb, tb), 0)
    col = lax.broadcasted_iota(jnp.int32, (tb, tb), 1)
    tri = jnp.where(col <= row, 1.0, 0.0).astype(BF16)
    s = carry_ref[...]
    for limb in _split3(x_ref[...]):
        s = s + jnp.dot(tri, limb, preferred_element_type=F32)
    carry_ref[...] = s[tb - 1:tb, :]
    c2 = s * LOG2E
    limbs_col = jnp.concatenate(_split3(c2), axis=1)
    limbs_row = [limb.astype(F32) for limb in _split3(c2.T)]

    sel_r = lax.broadcasted_iota(jnp.int32, (3 * LANES, HEAD_DIM), 0)
    sel_c = lax.broadcasted_iota(jnp.int32, (3 * LANES, HEAD_DIM), 1)
    lane = lax.broadcasted_iota(jnp.int32, (tb, HEAD_DIM), 1)
    ones_cols = jnp.where(lane < 3, 1.0, 0.0)
    sub = lax.broadcasted_iota(jnp.int32, (HEAD_DIM, tb), 0)
    ones_rows = jnp.where(sub < 3, 0.0, jnp.where(sub < 6, 1.0, 0.0))
    for h in range(N_HEADS):
        target = jnp.where(sel_c < 3, -1, jnp.where(sel_c < 6, (sel_c - 3) * LANES + h, -1))
        pick = jnp.where(sel_r == target, -1.0, 0.0).astype(BF16)
        kc = jnp.dot(limbs_col, pick, preferred_element_type=F32) + ones_cols
        kc_ref[h, 0] = kc.astype(BF16)
        qc = ones_rows
        for l in range(3):
            qc = jnp.where(sub == l, limbs_row[l][h:h + 1, :], qc)
        qc_ref[h] = qc.astype(BF16)


def _cumsum_operands(x, *, tb):
    m, n = x.shape
    return pl.pallas_call(
        functools.partial(_cumsum_kernel, tb=tb),
        grid=(m // tb,),
        in_specs=[pl.BlockSpec((tb, n), lambda i: (i, 0))],
        out_specs=[pl.BlockSpec((N_HEADS, HEAD_DIM, tb), lambda i: (0, 0, i)),
                   pl.BlockSpec((N_HEADS, 1, tb, HEAD_DIM), lambda i: (0, i, 0, 0))],
        out_shape=[jax.ShapeDtypeStruct((N_HEADS, HEAD_DIM, m), BF16),
                   jax.ShapeDtypeStruct((N_HEADS, m // tb, tb, HEAD_DIM), BF16)],
        scratch_shapes=[pltpu.VMEM((1, n), F32)],
        compiler_params=_params("arbitrary"),
        name="forget_cumsum",
    )(x)


def _conv_kernel(u_ref, halo_ref, w_ref, b_ref, lg_ref, lb_ref, o_ref, buf_ref, sh_ref, y_ref, *, ts, rc, rn):
    i = pl.program_id(0)
    span = ts + CONV_HALO - SUBLANES
    off = CONV_HALO - (CONV_K - 1)

    for lb in range(CONV_CH // LANES):
        lanes = slice(lb * LANES, (lb + 1) * LANES)
        halo = halo_ref[:, lanes]
        buf_ref[lb, 0:CONV_HALO, :] = jnp.where(i == 0, jnp.zeros_like(halo), halo)
        buf_ref[lb, CONV_HALO:, :] = u_ref[:, lanes]
        for r in range(1, SUBLANES):
            sh_ref[r - 1, lb] = buf_ref[lb, r:r + span, :]

    for lb in range(CONV_CH // LANES):
        lanes = slice(lb * LANES, (lb + 1) * LANES)

        def chunk(c, carry, lb=lb, lanes=lanes):
            base = pl.multiple_of(c * rc, rc)
            acc = [jnp.zeros((rc // SUBLANES, SUBLANES, LANES), F32) for _ in range(2)]
            for k in range(CONV_K):
                r = (off + k) % SUBLANES
                a = (off + k) - r
                win = (buf_ref[lb, pl.ds(base + a, rc), :] if r == 0
                       else sh_ref[r - 1, lb, pl.ds(base + a, rc), :])
                acc[k % 2] = acc[k % 2] + win.reshape(rc // SUBLANES, SUBLANES, LANES) * w_ref[k, :, lanes][None]
            y_ref[pl.ds(base, rc), lanes] = (acc[0] + acc[1]).reshape(rc, LANES)
            return carry

        lax.fori_loop(0, ts // rc, chunk, 0)

    def norm(c, carry):
        base = pl.multiple_of(c * rn, rn)
        acc = y_ref[pl.ds(base, rn), :] + b_ref[...]
        mu = jnp.mean(acc, axis=-1, keepdims=True)
        d = acc - mu
        var = jnp.mean(d * d, axis=-1, keepdims=True)
        y = d * lax.rsqrt(var + EPS) * lg_ref[...] + lb_ref[...]
        o_ref[pl.ds(base, rn), :] = (y * jax.nn.sigmoid(y)).astype(o_ref.dtype)
        return carry

    lax.fori_loop(0, ts // rn, norm, 0)


def _conformer_conv(u, conv_w, conv_b, ln_g, ln_b, *, ts=512, rc=128, rn=128):
    s, c = u.shape
    halo_blocks = ts // CONV_HALO
    conv_w = jnp.broadcast_to(conv_w[:, None, :], (CONV_K, SUBLANES, c))
    return pl.pallas_call(
        functools.partial(_conv_kernel, ts=ts, rc=rc, rn=rn),
        grid=(s // ts,),
        in_specs=[pl.BlockSpec((ts, c), lambda i: (i, 0)),
                  pl.BlockSpec((CONV_HALO, c), lambda i: (jnp.maximum(i * halo_blocks - 1, 0), 0)),
                  pl.BlockSpec((CONV_K, SUBLANES, c), lambda i: (0, 0, 0)),
                  pl.BlockSpec((1, c), lambda i: (0, 0)),
                  pl.BlockSpec((1, c), lambda i: (0, 0)),
                  pl.BlockSpec((1, c), lambda i: (0, 0))],
        out_specs=pl.BlockSpec((ts, c), lambda i: (i, 0)),
        out_shape=jax.ShapeDtypeStruct((s, c), BF16),
        scratch_shapes=[pltpu.VMEM((c // LANES, ts + CONV_HALO, LANES), F32),
                        pltpu.VMEM((SUBLANES - 1, c // LANES, ts + CONV_HALO - SUBLANES, LANES), F32),
                        pltpu.VMEM((ts, c), F32)],
        compiler_params=_params("parallel"),
        name="conformer_conv",
    )(u, u, conv_w, conv_b, ln_g, ln_b)


def _attn_kernel(qt_ref, qc_ref, k_ref, kc_ref, vt_ref, o_ref, qa_ref, ka_ref, m_ref, l_ref, acc_ref,
                 s0_ref, s1_ref, p0_ref, p1_ref, a0_ref, a1_ref, b0_ref, b1_ref, *, tq, heads):
    qi = pl.program_id(1)
    nk = k_ref.shape[1]
    s_refs, p_refs, a_refs, b_refs = (s0_ref, s1_ref), (p0_ref, p1_ref), (a0_ref, a1_ref), (b0_ref, b1_ref)

    @pl.when(qi == 0)
    def _():
        def widen(j, carry):
            for g in range(heads):
                ka_ref[g, j, :, :HEAD_DIM] = k_ref[g, j]
                ka_ref[g, j, :, HEAD_DIM:] = kc_ref[g, j]
            return carry
        lax.fori_loop(0, nk, widen, 0)

    qa_ref[:, :HEAD_DIM, :] = qt_ref[...]
    qa_ref[:, HEAD_DIM:, :] = qc_ref[...]
    m_ref[...] = jnp.full_like(m_ref, NEG_INF)
    l_ref[...] = jnp.zeros_like(l_ref)
    acc_ref[...] = jnp.zeros_like(acc_ref)
    for slot in range(2):
        p_refs[slot][...] = jnp.zeros_like(p_refs[slot])
        a_refs[slot][...] = jnp.ones_like(a_refs[slot])

    def scores(g, j, slot):
        s = jnp.dot(ka_ref[g, j], qa_ref[g], preferred_element_type=F32)
        s_refs[slot][g] = s
        b_refs[slot][g] = jnp.max(s, axis=0, keepdims=True)

    def weighted_values(g, j, slot):
        acc_ref[g] = a_refs[slot][g] * acc_ref[g] + jnp.dot(vt_ref[g, j], p_refs[slot][g],
                                                            preferred_element_type=F32)

    def softmax(g, slot, diagonal):
        s = s_refs[slot][g]
        if diagonal:
            key = lax.broadcasted_iota(jnp.int32, (tq, tq), 0)
            qry = lax.broadcasted_iota(jnp.int32, (tq, tq), 1)
            s = jnp.where(key <= qry, s, NEG_INF)
            block_max = jnp.max(s, axis=0, keepdims=True)
        else:
            block_max = b_refs[slot][g]
        m_prev = m_ref[g]
        m_new = jnp.maximum(m_prev, block_max)
        alpha = jnp.exp2(m_prev - m_new)
        p = jnp.exp2(s - m_new)
        l_ref[g] = alpha * l_ref[g] + jnp.sum(p, axis=0, keepdims=True)
        m_ref[g] = m_new
        p_refs[slot][g] = p.astype(BF16)
        a_refs[slot][g] = alpha

    def step(j, slot):
        for g in range(heads):
            scores(g, j + 1, 1 - slot)
        for g in range(heads):
            weighted_values(g, jnp.maximum(j - 1, 0), 1 - slot)
        for g in range(heads):
            softmax(g, slot, False)

    def last_step(slot):
        for g in range(heads):
            weighted_values(g, jnp.maximum(qi - 1, 0), 1 - slot)
        for g in range(heads):
            softmax(g, slot, True)
        for g in range(heads):
            weighted_values(g, qi, slot)

    for g in range(heads):
        scores(g, 0, 0)

    def pair(t, carry):
        step(2 * t, 0)
        step(2 * t + 1, 1)
        return carry

    lax.fori_loop(0, qi // 2, pair, 0)
    odd = qi % 2 == 1

    @pl.when(odd)
    def _():
        step(qi - 1, 0)
        last_step(1)

    @pl.when(jnp.logical_not(odd))
    def _():
        last_step(0)

    for g in range(heads):
        o_ref[:, g * HEAD_DIM:(g + 1) * HEAD_DIM] = (acc_ref[g] / l_ref[g]).T.astype(o_ref.dtype)


def _forgetting_attention(q_t, qc, k, kc, v_t, *, tq, heads=2):
    h, _, s = q_t.shape
    nk = s // tq
    q_spec = pl.BlockSpec((heads, HEAD_DIM, tq), lambda hg, qi: (hg, 0, qi))
    k_spec = pl.BlockSpec((heads, nk, tq, HEAD_DIM), lambda hg, qi: (hg, 0, 0, 0))
    return pl.pallas_call(
        functools.partial(_attn_kernel, tq=tq, heads=heads),
        grid=(h // heads, nk),
        in_specs=[q_spec, q_spec, k_spec, k_spec,
                  pl.BlockSpec((heads, nk, HEAD_DIM, tq), lambda hg, qi: (hg, 0, 0, 0))],
        out_specs=pl.BlockSpec((tq, heads * HEAD_DIM), lambda hg, qi: (qi, hg)),
        out_shape=jax.ShapeDtypeStruct((s, h * HEAD_DIM), BF16),
        scratch_shapes=[pltpu.VMEM((heads, MXU_DEPTH, tq), BF16),
                        pltpu.VMEM((heads, nk, tq, MXU_DEPTH), BF16),
                        pltpu.VMEM((heads, 1, tq), F32), pltpu.VMEM((heads, 1, tq), F32),
                        pltpu.VMEM((heads, HEAD_DIM, tq), F32),
                        pltpu.VMEM((heads, tq, tq), F32), pltpu.VMEM((heads, tq, tq), F32),
                        pltpu.VMEM((heads, tq, tq), BF16), pltpu.VMEM((heads, tq, tq), BF16),
                        pltpu.VMEM((heads, 1, tq), F32), pltpu.VMEM((heads, 1, tq), F32),
                        pltpu.VMEM((heads, 1, tq), F32), pltpu.VMEM((heads, 1, tq), F32)],
        compiler_params=_params("parallel", "arbitrary"),
        name="forgetting_attention",
    )(q_t, qc, k, kc, v_t)


def _merge_kernel(x_ref, ac_ref, at_ref, gc_ref, ga_ref, wc_ref, wa_ref, wo_ref, ng_ref, o_ref, h_ref):
    yc = jnp.dot(ac_ref[...], wc_ref[...], preferred_element_type=F32)
    ya = jnp.dot(at_ref[...], wa_ref[...], preferred_element_type=F32)
    merged = gc_ref[...].astype(F32) * yc + ga_ref[...].astype(F32) * ya
    y = x_ref[...] + jnp.dot(merged.astype(BF16), wo_ref[...], preferred_element_type=F32)
    o_ref[...] = y
    h_ref[...] = _rms(y, ng_ref[...]).astype(BF16)


def _merge(x, a_conv, a_attn, gates, w_conv_out, w_attn_out, w_out, ffn_g, *, layer, tm=512):
    s, d = x.shape
    const = lambda i: (layer, 0, 0)
    once = pl.Buffered(1)
    row_spec = pl.BlockSpec((tm, d), lambda i: (i, 0))
    return pl.pallas_call(
        _merge_kernel,
        grid=(s // tm,),
        in_specs=[row_spec,
                  pl.BlockSpec((tm, CONV_CH), lambda i: (i, 0)),
                  pl.BlockSpec((tm, ATTN_W), lambda i: (i, 0)),
                  pl.BlockSpec((tm, d), lambda i: (i, 0)),
                  pl.BlockSpec((tm, d), lambda i: (i, 1)),
                  pl.BlockSpec((None, CONV_CH, d), const, pipeline_mode=once),
                  pl.BlockSpec((None, ATTN_W, d), const, pipeline_mode=once),
                  pl.BlockSpec((None, d, d), const, pipeline_mode=once),
                  pl.BlockSpec((1, d), lambda i: (0, 0))],
        out_specs=[row_spec, row_spec],
        out_shape=[jax.ShapeDtypeStruct((s, d), F32), jax.ShapeDtypeStruct((s, d), BF16)],
        compiler_params=_params("parallel"),
        name="gated_merge",
    )(x, a_conv, a_attn, gates, gates, w_conv_out, w_attn_out, w_out, ffn_g)


def _ffn_kernel(h_ref, wg_ref, wu_ref, wd_ref, o_ref):
    @pl.when(pl.program_id(1) == 0)
    def _():
        o_ref[...] = jnp.zeros_like(o_ref)

    h = h_ref[...]
    gate = jnp.dot(h, wg_ref[...].astype(BF16), preferred_element_type=F32)
    up = jnp.dot(h, wu_ref[...].astype(BF16), preferred_element_type=F32)
    a = (gate * jax.nn.sigmoid(gate) * up).astype(BF16)
    o_ref[...] += jnp.dot(a, wd_ref[...].astype(BF16), preferred_element_type=F32)


def _ffn(h, w_gate_up, w_down, *, layer, tm=1024, th=256):
    s, d = h.shape
    nh = FFN_HIDDEN // th
    return pl.pallas_call(
        _ffn_kernel,
        grid=(s // tm, nh),
        in_specs=[pl.BlockSpec((tm, d), lambda i, c: (i, 0)),
                  pl.BlockSpec((None, d, th), lambda i, c: (layer, 0, c)),
                  pl.BlockSpec((None, d, th), lambda i, c: (layer, 0, c + nh)),
                  pl.BlockSpec((None, th, d), lambda i, c: (layer, c, 0))],
        out_specs=pl.BlockSpec((tm, d), lambda i, c: (i, 0)),
        out_shape=jax.ShapeDtypeStruct((s, d), F32),
        compiler_params=_params("parallel", "arbitrary"),
        name="swiglu_ffn",
    )(h, w_gate_up, w_gate_up, w_down)


def _ple_kernel(x_ref, f_ref, g_ref, p_ref, wg_ref, wp_ref, ng_ref, *o_refs, final):
    x = x_ref[...] + f_ref[...]
    h = _rms(x, g_ref[...]).astype(BF16)
    gate = jax.nn.sigmoid(jnp.dot(h, wg_ref[...], preferred_element_type=F32))
    emb = jnp.dot(p_ref[...].astype(BF16), wp_ref[...], preferred_element_type=F32)
    y = x + gate * emb
    if final:
        o_refs[0][...] = _rms(y, ng_ref[...])
    else:
        o_refs[0][...] = y
        o_refs[1][...] = _rms(y, ng_ref[...]).astype(BF16)


def _ple(x, f, g, p, w_gate, w_proj, next_g, *, layer, final, tm=256):
    s, d = x.shape
    pd = p.shape[-1]
    const = lambda i: (0, 0)
    stacked = lambda i: (layer, 0, 0)
    once = pl.Buffered(1)
    row_spec = pl.BlockSpec((tm, d), lambda i: (i, 0))
    out_specs = [row_spec] if final else [row_spec, row_spec]
    out_shape = [jax.ShapeDtypeStruct((s, d), F32)] + ([] if final else [jax.ShapeDtypeStruct((s, d), BF16)])
    return pl.pallas_call(
        functools.partial(_ple_kernel, final=final),
        grid=(s // tm,),
        in_specs=[row_spec,
                  row_spec,
                  pl.BlockSpec((1, d), const),
                  pl.BlockSpec((None, tm, pd), lambda i: (layer, i, 0)),
                  pl.BlockSpec((None, d, d), stacked, pipeline_mode=once),
                  pl.BlockSpec((None, pd, d), stacked, pipeline_mode=once),
                  pl.BlockSpec((1, d), const)],
        out_specs=out_specs,
        out_shape=out_shape,
        compiler_params=_params("parallel"),
        name="ple_final" if final else "ple",
    )(x, f, g, p, w_gate, w_proj, next_g)


def kernel(x, p, norm_mix_g, w_in, b_forget, conv_w, conv_b, conv_ln_g, conv_ln_b, w_conv_out,
           w_attn_out, w_out, norm_ffn_g, w_gate_up, w_down, norm_ple_g, w_ple_gate, w_ple_proj, final_g):
    b, s, d = x.shape
    depth = w_in.shape[0]
    assert b == 1 and d == D_MODEL
    xs = x.reshape(s, d)
    row = lambda v: v.reshape(1, -1)
    o_f = 2 * CONV_CH + 3 * ATTN_W
    o_g = o_f + N_HEADS
    tq = 512
    w_conv_out, w_attn_out, w_out, w_ple_gate, w_ple_proj = (
        w.astype(BF16) for w in (w_conv_out, w_attn_out, w_out, w_ple_gate, w_ple_proj))
    p = p.reshape(depth, s, -1)
    w_main = w_in[:, :, :o_f].astype(BF16)
    w_forget = jnp.pad(w_in[:, :, o_f:o_g], ((0, 0), (0, 0), (0, LANES - N_HEADS))).astype(BF16)
    w_gates = w_in[:, :, o_g:].astype(BF16)
    c_glu, c_qkv = 0, 2 * CONV_CH
    h = _rmsnorm_cast(xs, row(norm_mix_g[0]))

    for i in range(depth):
        b_f = jnp.pad(b_forget[i], (0, LANES - N_HEADS)).reshape(1, LANES)

        u = _in_proj(h, w_main, [c_glu // 512, (c_glu + CONV_CH) // 512], [], _ep_glu, F32,
                     layer=i, n=CONV_CH, tm=1024, tn=512, name="in_proj_glu")
        q_t, k, v_t = _qkv_proj(h, w_main, c_qkv // ATTN_W, layer=i, tm=1024, tq=tq)
        gates = _in_proj(h, w_gates, [0], [], _ep_sigmoid, BF16,
                         layer=i, n=2 * D_MODEL, tm=1024, tn=1024, name="in_proj_gates")
        log_f = _in_proj(h, w_forget, [0], [b_f], _ep_log_forget, F32,
                         layer=i, n=LANES, tm=1024, tn=LANES, name="in_proj_forget")

        qc, kc = _cumsum_operands(log_f, tb=tq)
        a_attn = _forgetting_attention(q_t, qc, k, kc, v_t, tq=tq)

        a_conv = _conformer_conv(u, conv_w[i], row(conv_b[i]), row(conv_ln_g[i]), row(conv_ln_b[i]))

        xs, h_ffn = _merge(xs, a_conv, a_attn, gates, w_conv_out, w_attn_out, w_out, row(norm_ffn_g[i]), layer=i)
        f = _ffn(h_ffn, w_gate_up, w_down, layer=i)
        if i == depth - 1:
            (xs,) = _ple(xs, f, row(norm_ple_g[i]), p, w_ple_gate, w_ple_proj, row(final_g), layer=i, final=True)
        else:
            xs, h = _ple(xs, f, row(norm_ple_g[i]), p, w_ple_gate, w_ple_proj, row(norm_mix_g[i + 1]),
                         layer=i, final=False)
    return xs.reshape(b, s, d)
```

```python
import functools
import math

import jax
import jax.numpy as jnp
from jax import lax
from jax.experimental import pallas as pl
from jax.experimental.pallas import tpu as pltpu

D_MODEL = 2048
N_HEADS = 8
HEAD_DIM = 128
ATTN_W = N_HEADS * HEAD_DIM
CONV_CH = D_MODEL // 2
CONV_K = 31
FFN_HIDDEN = 5632
EPS = 1e-6
NEG_INF = -1e30
LOG2E = math.log2(math.e)

LANES = 128
SUBLANES = 8
MXU_DEPTH = 256
CONV_HALO = 32
VMEM_LIMIT_BYTES = 56 * 1024 * 1024

BF16 = jnp.bfloat16
F32 = jnp.float32


def _params(*sem):
    return pltpu.CompilerParams(dimension_semantics=sem, vmem_limit_bytes=VMEM_LIMIT_BYTES)


def _rms(x, g):
    ms = jnp.mean(x * x, axis=-1, keepdims=True)
    return x * lax.rsqrt(ms + EPS) * g


def _dot_nt(a, b_t):
    return lax.dot_general(a, b_t, (((1,), (1,)), ((), ())), preferred_element_type=F32)


def _split3(x):
    hi = x.astype(BF16)
    r1 = x - hi.astype(F32)
    mid = r1.astype(BF16)
    lo = (r1 - mid.astype(F32)).astype(BF16)
    return hi, mid, lo


def _rmsnorm_kernel(x_ref, g_ref, o_ref):
    o_ref[...] = _rms(x_ref[...], g_ref[...]).astype(o_ref.dtype)


def _rmsnorm_cast(x, g, *, tm=512):
    m, k = x.shape
    return pl.pallas_call(
        _rmsnorm_kernel,
        grid=(m // tm,),
        in_specs=[pl.BlockSpec((tm, k), lambda i: (i, 0)), pl.BlockSpec((1, k), lambda i: (0, 0))],
        out_specs=pl.BlockSpec((tm, k), lambda i: (i, 0)),
        out_shape=jax.ShapeDtypeStruct((m, k), BF16),
        compiler_params=_params("parallel"),
        name="mix_norm",
    )(x, g)


def _mm_kernel(*refs, n_w, n_extra, epilogue):
    h_ref = refs[0]
    w_refs = refs[1:1 + n_w]
    e_refs = refs[1 + n_w:1 + n_w + n_extra]
    o_ref = refs[1 + n_w + n_extra]
    h = h_ref[...]
    accs = [_dot_nt(h, w[...]) for w in w_refs]
    o_ref[...] = epilogue(accs, [e[...] for e in e_refs]).astype(o_ref.dtype)


def _in_proj(h, w_t, col_tiles, extras, epilogue, out_dtype, *, layer, n, tm, tn, name):
    m, k = h.shape
    grid = (m // tm, n // tn)
    in_specs = [pl.BlockSpec((tm, k), lambda i, j: (i, 0))]
    in_specs += [pl.BlockSpec((None, tn, k), functools.partial(lambda i, j, c: (layer, j + c, 0), c=c))
                 for c in col_tiles]
    in_specs += [pl.BlockSpec((1, tn), lambda i, j: (0, j)) for _ in extras]
    return pl.pallas_call(
        functools.partial(_mm_kernel, n_w=len(col_tiles), n_extra=len(extras), epilogue=epilogue),
        grid=grid,
        in_specs=in_specs,
        out_specs=pl.BlockSpec((tm, tn), lambda i, j: (i, j)),
        out_shape=jax.ShapeDtypeStruct((m, n), out_dtype),
        compiler_params=_params("parallel", "arbitrary"),
        name=name,
    )(h, *([w_t] * len(col_tiles)), *extras)


def _ep_sigmoid(accs, extras):
    return jax.nn.sigmoid(accs[0])


def _ep_glu(accs, extras):
    return accs[0] * jax.nn.sigmoid(accs[1])


def _ep_log_forget(accs, extras):
    return jax.nn.log_sigmoid(accs[0] + extras[0])


def _qkv_kernel(h_ref, w_ref, qt_ref, k_ref, vt_ref, *, tq, q_scale):
    j = pl.program_id(1)
    acc = _dot_nt(h_ref[...], w_ref[...])
    head = lambda h, r: acc[r * tq:(r + 1) * tq, h * HEAD_DIM:(h + 1) * HEAD_DIM]
    halves = k_ref.shape[1]

    @pl.when(j == 0)
    def _():
        for h in range(N_HEADS):
            for r in range(halves):
                qt_ref[h, :, r * tq:(r + 1) * tq] = (head(h, r) * q_scale).T.astype(BF16)

    @pl.when(j == 1)
    def _():
        for h in range(N_HEADS):
            for r in range(halves):
                k_ref[h, r] = head(h, r).astype(BF16)

    @pl.when(j == 2)
    def _():
        for h in range(N_HEADS):
            for r in range(halves):
                vt_ref[h, r] = head(h, r).T.astype(BF16)


def _qkv_proj(h, w, col_tile, *, layer, tm, tq):
    m, k = h.shape
    nk = m // tq
    halves = tm // tq
    q_scale = LOG2E / math.sqrt(HEAD_DIM)
    return pl.pallas_call(
        functools.partial(_qkv_kernel, tq=tq, q_scale=q_scale),
        grid=(m // tm, 3),
        in_specs=[pl.BlockSpec((tm, k), lambda i, j: (i, 0)),
                  pl.BlockSpec((None, ATTN_W, k), lambda i, j: (layer, j + col_tile, 0))],
        out_specs=[pl.BlockSpec((N_HEADS, HEAD_DIM, tm), lambda i, j: (0, 0, i)),
                   pl.BlockSpec((N_HEADS, halves, tq, HEAD_DIM), lambda i, j: (0, i, 0, 0)),
                   pl.BlockSpec((N_HEADS, halves, HEAD_DIM, tq), lambda i, j: (0, i, 0, 0))],
        out_shape=[jax.ShapeDtypeStruct((N_HEADS, HEAD_DIM, m), BF16),
                   jax.ShapeDtypeStruct((N_HEADS, nk, tq, HEAD_DIM), BF16),
                   jax.ShapeDtypeStruct((N_HEADS, nk, HEAD_DIM, tq), BF16)],
        compiler_params=_params("parallel", "arbitrary"),
        name="in_proj_qkv",
    )(h, w)


def _cumsum_kernel(x_ref, qc_ref, kc_ref, carry_ref, *, tb):
    @pl.when(pl.program_id(0) == 0)
    def _():
        carry_ref[...] = jnp.zeros_like(carry_ref)

    row = lax.broadcasted_iota(jnp.int32, (tb, tb), 0)
    col = lax.broadcasted_iota(jnp.int32, (tb, tb), 1)
    tri = jnp.where(col <= row, 1.0, 0.0).astype(BF16)
    s = carry_ref[...]
    for limb in _split3(x_ref[...]):
        s = s + jnp.dot(tri, limb, preferred_element_type=F32)
    carry_ref[...] = s[tb - 1:tb, :]
    c2 = s * LOG2E
    limbs_col = jnp.concatenate(_split3(c2), axis=1)
    limbs_row = [limb.astype(F32) for limb in _split3(c2.T)]

    sel_r = lax.broadcasted_iota(jnp.int32, (3 * LANES, HEAD_DIM), 0)
    sel_c = lax.broadcasted_iota(jnp.int32, (3 * LANES, HEAD_DIM), 1)
    lane = lax.broadcasted_iota(jnp.int32, (tb, HEAD_DIM), 1)
    ones_cols = jnp.where(lane < 3, 1.0, 0.0)
    sub = lax.broadcasted_iota(jnp.int32, (HEAD_DIM, tb), 0)
    ones_rows = jnp.where(sub < 3, 0.0, jnp.where(sub < 6, 1.0, 0.0))
    for h in range(N_HEADS):
        target = jnp.where(sel_c < 3, -1, jnp.where(sel_c < 6, (sel_c - 3) * LANES + h, -1))
        pick = jnp.where(sel_r == target, -1.0, 0.0).astype(BF16)
        kc = jnp.dot(limbs_col, pick, preferred_element_type=F32) + ones_cols
        kc_ref[h, 0] = kc.astype(BF16)
        qc = ones_rows
        for l in range(3):
            qc = jnp.where(sub == l, limbs_row[l][h:h + 1, :], qc)
        qc_ref[h] = qc.astype(BF16)


def _cumsum_operands(x, *, tb):
    m, n = x.shape
    return pl.pallas_call(
        functools.partial(_cumsum_kernel, tb=tb),
        grid=(m // tb,),
        in_specs=[pl.BlockSpec((tb, n), lambda i: (i, 0))],
        out_specs=[pl.BlockSpec((N_HEADS, HEAD_DIM, tb), lambda i: (0, 0, i)),
                   pl.BlockSpec((N_HEADS, 1, tb, HEAD_DIM), lambda i: (0, i, 0, 0))],
        out_shape=[jax.ShapeDtypeStruct((N_HEADS, HEAD_DIM, m), BF16),
                   jax.ShapeDtypeStruct((N_HEADS, m // tb, tb, HEAD_DIM), BF16)],
        scratch_shapes=[pltpu.VMEM((1, n), F32)],
        compiler_params=_params("arbitrary"),
        name="forget_cumsum",
    )(x)


def _conv_kernel(u_ref, halo_ref, w_ref, b_ref, lg_ref, lb_ref, o_ref, buf_ref, sh_ref, y_ref, *, ts, rc, rn):
    i = pl.program_id(0)
    span = ts + CONV_HALO - SUBLANES
    off = CONV_HALO - (CONV_K - 1)

    for lb in range(CONV_CH // LANES):
        lanes = slice(lb * LANES, (lb + 1) * LANES)
        halo = halo_ref[:, lanes]
        buf_ref[lb, 0:CONV_HALO, :] = jnp.where(i == 0, jnp.zeros_like(halo), halo)
        buf_ref[lb, CONV_HALO:, :] = u_ref[:, lanes]
        for r in range(1, SUBLANES):
            sh_ref[r - 1, lb] = buf_ref[lb, r:r + span, :]

    for lb in range(CONV_CH // LANES):
        lanes = slice(lb * LANES, (lb + 1) * LANES)

        def chunk(c, carry, lb=lb, lanes=lanes):
            base = pl.multiple_of(c * rc, rc)
            acc = [jnp.zeros((rc // SUBLANES, SUBLANES, LANES), F32) for _ in range(2)]
            for k in range(CONV_K):
                r = (off + k) % SUBLANES
                a = (off + k) - r
                win = (buf_ref[lb, pl.ds(base + a, rc), :] if r == 0
                       else sh_ref[r - 1, lb, pl.ds(base + a, rc), :])
                acc[k % 2] = acc[k % 2] + win.reshape(rc // SUBLANES, SUBLANES, LANES) * w_ref[k, :, lanes][None]
            y_ref[pl.ds(base, rc), lanes] = (acc[0] + acc[1]).reshape(rc, LANES)
            return carry

        lax.fori_loop(0, ts // rc, chunk, 0)

    def norm(c, carry):
        base = pl.multiple_of(c * rn, rn)
        acc = y_ref[pl.ds(base, rn), :] + b_ref[...]
        mu = jnp.mean(acc, axis=-1, keepdims=True)
        d = acc - mu
        var = jnp.mean(d * d, axis=-1, keepdims=True)
        y = d * lax.rsqrt(var + EPS) * lg_ref[...] + lb_ref[...]
        o_ref[pl.ds(base, rn), :] = (y * jax.nn.sigmoid(y)).astype(o_ref.dtype)
        return carry

    lax.fori_loop(0, ts // rn, norm, 0)


def _conformer_conv(u, conv_w, conv_b, ln_g, ln_b, *, ts=512, rc=128, rn=128):
    s, c = u.shape
    halo_blocks = ts // CONV_HALO
    conv_w = jnp.broadcast_to(conv_w[:, None, :], (CONV_K, SUBLANES, c))
    return pl.pallas_call(
        functools.partial(_conv_kernel, ts=ts, rc=rc, rn=rn),
        grid=(s // ts,),
        in_specs=[pl.BlockSpec((ts, c), lambda i: (i, 0)),
                  pl.BlockSpec((CONV_HALO, c), lambda i: (jnp.maximum(i * halo_blocks - 1, 0), 0)),
                  pl.BlockSpec((CONV_K, SUBLANES, c), lambda i: (0, 0, 0)),
                  pl.BlockSpec((1, c), lambda i: (0, 0)),
                  pl.BlockSpec((1, c), lambda i: (0, 0)),
                  pl.BlockSpec((1, c), lambda i: (0, 0))],
        out_specs=pl.BlockSpec((ts, c), lambda i: (i, 0)),
        out_shape=jax.ShapeDtypeStruct((s, c), BF16),
        scratch_shapes=[pltpu.VMEM((c // LANES, ts + CONV_HALO, LANES), F32),
                        pltpu.VMEM((SUBLANES - 1, c // LANES, ts + CONV_HALO - SUBLANES, LANES), F32),
                        pltpu.VMEM((ts, c), F32)],
        compiler_params=_params("parallel"),
        name="conformer_conv",
    )(u, u, conv_w, conv_b, ln_g, ln_b)


def _attn_kernel(qt_ref, qc_ref, k_ref, kc_ref, vt_ref, o_ref, qa_ref, ka_ref, m_ref, l_ref, acc_ref,
                 s0_ref, s1_ref, p0_ref, p1_ref, a0_ref, a1_ref, b0_ref, b1_ref, *, tq, heads):
    qi = pl.program_id(1)
    nk = k_ref.shape[1]
    s_refs, p_refs, a_refs, b_refs = (s0_ref, s1_ref), (p0_ref, p1_ref), (a0_ref, a1_ref), (b0_ref, b1_ref)

    @pl.when(qi == 0)
    def _():
        def widen(j, carry):
            for g in range(heads):
                ka_ref[g, j, :, :HEAD_DIM] = k_ref[g, j]
                ka_ref[g, j, :, HEAD_DIM:] = kc_ref[g, j]
            return carry
        lax.fori_loop(0, nk, widen, 0)

    qa_ref[:, :HEAD_DIM, :] = qt_ref[...]
    qa_ref[:, HEAD_DIM:, :] = qc_ref[...]
    m_ref[...] = jnp.full_like(m_ref, NEG_INF)
    l_ref[...] = jnp.zeros_like(l_ref)
    acc_ref[...] = jnp.zeros_like(acc_ref)
    for slot in range(2):
        p_refs[slot][...] = jnp.zeros_like(p_refs[slot])
        a_refs[slot][...] = jnp.ones_like(a_refs[slot])

    def scores(g, j, slot):
        s = jnp.dot(ka_ref[g, j], qa_ref[g], preferred_element_type=F32)
        s_refs[slot][g] = s
        b_refs[slot][g] = jnp.max(s, axis=0, keepdims=True)

    def weighted_values(g, j, slot):
        acc_ref[g] = a_refs[slot][g] * acc_ref[g] + jnp.dot(vt_ref[g, j], p_refs[slot][g],
                                                            preferred_element_type=F32)

    def softmax(g, slot, diagonal):
        s = s_refs[slot][g]
        if diagonal:
            key = lax.broadcasted_iota(jnp.int32, (tq, tq), 0)
            qry = lax.broadcasted_iota(jnp.int32, (tq, tq), 1)
            s = jnp.where(key <= qry, s, NEG_INF)
            block_max = jnp.max(s, axis=0, keepdims=True)
        else:
            block_max = b_refs[slot][g]
        m_prev = m_ref[g]
        m_new = jnp.maximum(m_prev, block_max)
        alpha = jnp.exp2(m_prev - m_new)
        p = jnp.exp2(s - m_new)
        l_ref[g] = alpha * l_ref[g] + jnp.sum(p, axis=0, keepdims=True)
        m_ref[g] = m_new
        p_refs[slot][g] = p.astype(BF16)
        a_refs[slot][g] = alpha

    def step(j, slot):
        for g in range(heads):
            scores(g, j + 1, 1 - slot)
        for g in range(heads):
            weighted_values(g, jnp.maximum(j - 1, 0), 1 - slot)
        for g in range(heads):
            softmax(g, slot, False)

    def last_step(slot):
        for g in range(heads):
            weighted_values(g, jnp.maximum(qi - 1, 0), 1 - slot)
        for g in range(heads):
            softmax(g, slot, True)
        for g in range(heads):
            weighted_values(g, qi, slot)

    for g in range(heads):
        scores(g, 0, 0)

    def pair(t, carry):
        step(2 * t, 0)
        step(2 * t + 1, 1)
        return carry

    lax.fori_loop(0, qi // 2, pair, 0)
    odd = qi % 2 == 1

    @pl.when(odd)
    def _():
        step(qi - 1, 0)
        last_step(1)

    @pl.when(jnp.logical_not(odd))
    def _():
        last_step(0)

    for g in range(heads):
        o_ref[:, g * HEAD_DIM:(g + 1) * HEAD_DIM] = (acc_ref[g] / l_ref[g]).T.astype(o_ref.dtype)


def _forgetting_attention(q_t, qc, k, kc, v_t, *, tq, heads=2):
    h, _, s = q_t.shape
    nk = s // tq
    q_spec = pl.BlockSpec((heads, HEAD_DIM, tq), lambda hg, qi: (hg, 0, qi))
    k_spec = pl.BlockSpec((heads, nk, tq, HEAD_DIM), lambda hg, qi: (hg, 0, 0, 0))
    return pl.pallas_call(
        functools.partial(_attn_kernel, tq=tq, heads=heads),
        grid=(h // heads, nk),
        in_specs=[q_spec, q_spec, k_spec, k_spec,
                  pl.BlockSpec((heads, nk, HEAD_DIM, tq), lambda hg, qi: (hg, 0, 0, 0))],
        out_specs=pl.BlockSpec((tq, heads * HEAD_DIM), lambda hg, qi: (qi, hg)),
        out_shape=jax.ShapeDtypeStruct((s, h * HEAD_DIM), BF16),
        scratch_shapes=[pltpu.VMEM((heads, MXU_DEPTH, tq), BF16),
                        pltpu.VMEM((heads, nk, tq, MXU_DEPTH), BF16),
                        pltpu.VMEM((heads, 1, tq), F32), pltpu.VMEM((heads, 1, tq), F32),
                        pltpu.VMEM((heads, HEAD_DIM, tq), F32),
                        pltpu.VMEM((heads, tq, tq), F32), pltpu.VMEM((heads, tq, tq), F32),
                        pltpu.VMEM((heads, tq, tq), BF16), pltpu.VMEM((heads, tq, tq), BF16),
                        pltpu.VMEM((heads, 1, tq), F32), pltpu.VMEM((heads, 1, tq), F32),
                        pltpu.VMEM((heads, 1, tq), F32), pltpu.VMEM((heads, 1, tq), F32)],
        compiler_params=_params("parallel", "arbitrary"),
        name="forgetting_attention",
    )(q_t, qc, k, kc, v_t)


def _merge_kernel(x_ref, ac_ref, at_ref, gc_ref, ga_ref, wc_ref, wa_ref, wo_ref, ng_ref, o_ref, h_ref):
    yc = jnp.dot(ac_ref[...], wc_ref[...], preferred_element_type=F32)
    ya = jnp.dot(at_ref[...], wa_ref[...], preferred_element_type=F32)
    merged = gc_ref[...].astype(F32) * yc + ga_ref[...].astype(F32) * ya
    y = x_ref[...] + jnp.dot(merged.astype(BF16), wo_ref[...], preferred_element_type=F32)
    o_ref[...] = y
    h_ref[...] = _rms(y, ng_ref[...]).astype(BF16)


def _merge(x, a_conv, a_attn, gates, w_conv_out, w_attn_out, w_out, ffn_g, *, layer, tm=512):
    s, d = x.shape
    const = lambda i: (layer, 0, 0)
    once = pl.Buffered(1)
    row_spec = pl.BlockSpec((tm, d), lambda i: (i, 0))
    return pl.pallas_call(
        _merge_kernel,
        grid=(s // tm,),
        in_specs=[row_spec,
                  pl.BlockSpec((tm, CONV_CH), lambda i: (i, 0)),
                  pl.BlockSpec((tm, ATTN_W), lambda i: (i, 0)),
                  pl.BlockSpec((tm, d), lambda i: (i, 0)),
                  pl.BlockSpec((tm, d), lambda i: (i, 1)),
                  pl.BlockSpec((None, CONV_CH, d), const, pipeline_mode=once),
                  pl.BlockSpec((None, ATTN_W, d), const, pipeline_mode=once),
                  pl.BlockSpec((None, d, d), const, pipeline_mode=once),
                  pl.BlockSpec((1, d), lambda i: (0, 0))],
        out_specs=[row_spec, row_spec],
        out_shape=[jax.ShapeDtypeStruct((s, d), F32), jax.ShapeDtypeStruct((s, d), BF16)],
        compiler_params=_params("parallel"),
        name="gated_merge",
    )(x, a_conv, a_attn, gates, gates, w_conv_out, w_attn_out, w_out, ffn_g)


def _ffn_kernel(h_ref, wg_ref, wu_ref, wd_ref, o_ref):
    @pl.when(pl.program_id(1) == 0)
    def _():
        o_ref[...] = jnp.zeros_like(o_ref)

    h = h_ref[...]
    gate = jnp.dot(h, wg_ref[...].astype(BF16), preferred_element_type=F32)
    up = jnp.dot(h, wu_ref[...].astype(BF16), preferred_element_type=F32)
    a = (gate * jax.nn.sigmoid(gate) * up).astype(BF16)
    o_ref[...] += jnp.dot(a, wd_ref[...].astype(BF16), preferred_element_type=F32)


def _ffn(h, w_gate_up, w_down, *, layer, tm=1024, th=256):
    s, d = h.shape
    nh = FFN_HIDDEN // th
    return pl.pallas_call(
        _ffn_kernel,
        grid=(s // tm, nh),
        in_specs=[pl.BlockSpec((tm, d), lambda i, c: (i, 0)),
                  pl.BlockSpec((None, d, th), lambda i, c: (layer, 0, c)),
                  pl.BlockSpec((None, d, th), lambda i, c: (layer, 0, c + nh)),
                  pl.BlockSpec((None, th, d), lambda i, c: (layer, c, 0))],
        out_specs=pl.BlockSpec((tm, d), lambda i, c: (i, 0)),
        out_shape=jax.ShapeDtypeStruct((s, d), F32),
        compiler_params=_params("parallel", "arbitrary"),
        name="swiglu_ffn",
    )(h, w_gate_up, w_gate_up, w_down)


def _ple_kernel(x_ref, f_ref, g_ref, p_ref, wg_ref, wp_ref, ng_ref, *o_refs, final):
    x = x_ref[...] + f_ref[...]
    h = _rms(x, g_ref[...]).astype(BF16)
    gate = jax.nn.sigmoid(jnp.dot(h, wg_ref[...], preferred_element_type=F32))
    emb = jnp.dot(p_ref[...].astype(BF16), wp_ref[...], preferred_element_type=F32)
    y = x + gate * emb
    if final:
        o_refs[0][...] = _rms(y, ng_ref[...])
    else:
        o_refs[0][...] = y
        o_refs[1][...] = _rms(y, ng_ref[...]).astype(BF16)


def _ple(x, f, g, p, w_gate, w_proj, next_g, *, layer, final, tm=256):
    s, d = x.shape
    pd = p.shape[-1]
    const = lambda i: (0, 0)
    stacked = lambda i: (layer, 0, 0)
    once = pl.Buffered(1)
    row_spec = pl.BlockSpec((tm, d), lambda i: (i, 0))
    out_specs = [row_spec] if final else [row_spec, row_spec]
    out_shape = [jax.ShapeDtypeStruct((s, d), F32)] + ([] if final else [jax.ShapeDtypeStruct((s, d), BF16)])
    return pl.pallas_call(
        functools.partial(_ple_kernel, final=final),
        grid=(s // tm,),
        in_specs=[row_spec,
                  row_spec,
                  pl.BlockSpec((1, d), const),
                  pl.BlockSpec((None, tm, pd), lambda i: (layer, i, 0)),
                  pl.BlockSpec((None, d, d), stacked, pipeline_mode=once),
                  pl.BlockSpec((None, pd, d), stacked, pipeline_mode=once),
                  pl.BlockSpec((1, d), const)],
        out_specs=out_specs,
        out_shape=out_shape,
        compiler_params=_params("parallel"),
        name="ple_final" if final else "ple",
    )(x, f, g, p, w_gate, w_proj, next_g)


def kernel(x, p, norm_mix_g, w_in, b_forget, conv_w, conv_b, conv_ln_g, conv_ln_b, w_conv_out,
           w_attn_out, w_out, norm_ffn_g, w_gate_up, w_down, norm_ple_g, w_ple_gate, w_ple_proj, final_g):
    b, s, d = x.shape
    depth = w_in.shape[0]
    assert b == 1 and d == D_MODEL
    xs = x.reshape(s, d)
    row = lambda v: v.reshape(1, -1)
    o_f = 2 * CONV_CH + 3 * ATTN_W
    o_g = o_f + N_HEADS
    tq = 512
    w_conv_out, w_attn_out, w_out, w_ple_gate, w_ple_proj = (
        w.astype(BF16) for w in (w_conv_out, w_attn_out, w_out, w_ple_gate, w_ple_proj))
    p = p.reshape(depth, s, -1)
    w_t = jnp.swapaxes(w_in, 1, 2).astype(BF16)
    w_gates = w_t[:, o_g:]
    c_glu, c_qkv = 0, 2 * CONV_CH
    h = _rmsnorm_cast(xs, row(norm_mix_g[0]))

    for i in range(depth):
        b_f = jnp.pad(b_forget[i], (0, LANES - N_HEADS)).reshape(1, LANES)

        u = _in_proj(h, w_t, [c_glu // 512, (c_glu + CONV_CH) // 512], [], _ep_glu, F32,
                     layer=i, n=CONV_CH, tm=1024, tn=512, name="in_proj_glu")
        q_t, k, v_t = _qkv_proj(h, w_t, c_qkv // ATTN_W, layer=i, tm=1024, tq=tq)
        gates = _in_proj(h, w_gates, [0], [], _ep_sigmoid, BF16,
                         layer=i, n=2 * D_MODEL, tm=1024, tn=1024, name="in_proj_gates")
        log_f = _in_proj(h, w_t, [o_f // LANES], [b_f], _ep_log_forget, F32,
                         layer=i, n=LANES, tm=1024, tn=LANES, name="in_proj_forget")

        qc, kc = _cumsum_operands(log_f, tb=tq)
        a_attn = _forgetting_attention(q_t, qc, k, kc, v_t, tq=tq)

        a_conv = _conformer_conv(u, conv_w[i], row(conv_b[i]), row(conv_ln_g[i]), row(conv_ln_b[i]))

        xs, h_ffn = _merge(xs, a_conv, a_attn, gates, w_conv_out, w_attn_out, w_out, row(norm_ffn_g[i]), layer=i)
        f = _ffn(h_ffn, w_gate_up, w_down, layer=i)
        if i == depth - 1:
            (xs,) = _ple(xs, f, row(norm_ple_g[i]), p, w_ple_gate, w_ple_proj, row(final_g), layer=i, final=True)
        else:
            xs, h = _ple(xs, f, row(norm_ple_g[i]), p, w_ple_gate, w_ple_proj, row(norm_mix_g[i + 1]),
                         layer=i, final=False)
    return xs.reshape(b, s, d)
```

```python
import functools
import math

import jax
import jax.numpy as jnp
from jax import lax
from jax.experimental import pallas as pl
from jax.experimental.pallas import tpu as pltpu

D_MODEL = 2048
N_HEADS = 8
HEAD_DIM = 128
ATTN_W = N_HEADS * HEAD_DIM
CONV_CH = D_MODEL // 2
CONV_K = 31
FFN_HIDDEN = 5632
EPS = 1e-6
NEG_INF = -1e30
LOG2E = math.log2(math.e)

LANES = 128
SUBLANES = 8
MXU_DEPTH = 256
CONV_HALO = 32
VMEM_LIMIT_BYTES = 56 * 1024 * 1024

BF16 = jnp.bfloat16
F32 = jnp.float32


def _params(*sem):
    return pltpu.CompilerParams(dimension_semantics=sem, vmem_limit_bytes=VMEM_LIMIT_BYTES)


def _rms(x, g):
    ms = jnp.mean(x * x, axis=-1, keepdims=True)
    return x * lax.rsqrt(ms + EPS) * g


def _dot_nt(a, b_t):
    return lax.dot_general(a, b_t, (((1,), (1,)), ((), ())), preferred_element_type=F32)


def _split3(x):
    hi = x.astype(BF16)
    r1 = x - hi.astype(F32)
    mid = r1.astype(BF16)
    lo = (r1 - mid.astype(F32)).astype(BF16)
    return hi, mid, lo


def _rmsnorm_kernel(x_ref, g_ref, o_ref):
    o_ref[...] = _rms(x_ref[...], g_ref[...]).astype(o_ref.dtype)


def _rmsnorm_cast(x, g, *, tm=512):
    m, k = x.shape
    return pl.pallas_call(
        _rmsnorm_kernel,
        grid=(m // tm,),
        in_specs=[pl.BlockSpec((tm, k), lambda i: (i, 0)), pl.BlockSpec((1, k), lambda i: (0, 0))],
        out_specs=pl.BlockSpec((tm, k), lambda i: (i, 0)),
        out_shape=jax.ShapeDtypeStruct((m, k), BF16),
        compiler_params=_params("parallel"),
        name="mix_norm",
    )(x, g)


def _mm_kernel(*refs, n_w, n_extra, epilogue):
    h_ref = refs[0]
    w_refs = refs[1:1 + n_w]
    e_refs = refs[1 + n_w:1 + n_w + n_extra]
    o_ref = refs[1 + n_w + n_extra]
    h = h_ref[...]
    accs = [_dot_nt(h, w[0].astype(BF16)) for w in w_refs]
    o_ref[...] = epilogue(accs, [e[...] for e in e_refs]).astype(o_ref.dtype)


def _in_proj(h, w_t, row_starts, extras, epilogue, out_dtype, *, layer, n, tm, tn, name):
    m, k = h.shape
    grid = (m // tm, n // tn)
    assert all(r % SUBLANES == 0 for r in row_starts)
    in_specs = [pl.BlockSpec((tm, k), lambda i, j: (i, 0))]
    in_specs += [pl.BlockSpec((pl.Element(1), pl.Element(tn), pl.Element(k)),
                              functools.partial(lambda i, j, r: (layer, pl.multiple_of(r + j * tn, SUBLANES), 0), r=r))
                 for r in row_starts]
    in_specs += [pl.BlockSpec((1, tn), lambda i, j: (0, j)) for _ in extras]
    return pl.pallas_call(
        functools.partial(_mm_kernel, n_w=len(row_starts), n_extra=len(extras), epilogue=epilogue),
        grid=grid,
        in_specs=in_specs,
        out_specs=pl.BlockSpec((tm, tn), lambda i, j: (i, j)),
        out_shape=jax.ShapeDtypeStruct((m, n), out_dtype),
        compiler_params=_params("parallel", "arbitrary"),
        name=name,
    )(h, *([w_t] * len(row_starts)), *extras)


def _ep_sigmoid(accs, extras):
    return jax.nn.sigmoid(accs[0])


def _ep_glu(accs, extras):
    return accs[0] * jax.nn.sigmoid(accs[1])


def _ep_log_forget(accs, extras):
    return jax.nn.log_sigmoid(accs[0] + extras[0])


def _qkv_kernel(h_ref, w_ref, qt_ref, k_ref, vt_ref, *, tq, q_scale):
    j = pl.program_id(1)
    acc = _dot_nt(h_ref[...], w_ref[...].astype(BF16))
    head = lambda h, r: acc[r * tq:(r + 1) * tq, h * HEAD_DIM:(h + 1) * HEAD_DIM]
    halves = k_ref.shape[1]

    @pl.when(j == 0)
    def _():
        for h in range(N_HEADS):
            for r in range(halves):
                qt_ref[h, :, r * tq:(r + 1) * tq] = (head(h, r) * q_scale).T.astype(BF16)

    @pl.when(j == 1)
    def _():
        for h in range(N_HEADS):
            for r in range(halves):
                k_ref[h, r] = head(h, r).astype(BF16)

    @pl.when(j == 2)
    def _():
        for h in range(N_HEADS):
            for r in range(halves):
                vt_ref[h, r] = head(h, r).T.astype(BF16)


def _qkv_proj(h, w, col_tile, *, layer, tm, tq):
    m, k = h.shape
    nk = m // tq
    halves = tm // tq
    q_scale = LOG2E / math.sqrt(HEAD_DIM)
    return pl.pallas_call(
        functools.partial(_qkv_kernel, tq=tq, q_scale=q_scale),
        grid=(m // tm, 3),
        in_specs=[pl.BlockSpec((tm, k), lambda i, j: (i, 0)),
                  pl.BlockSpec((None, ATTN_W, k), lambda i, j: (layer, j + col_tile, 0))],
        out_specs=[pl.BlockSpec((N_HEADS, HEAD_DIM, tm), lambda i, j: (0, 0, i)),
                   pl.BlockSpec((N_HEADS, halves, tq, HEAD_DIM), lambda i, j: (0, i, 0, 0)),
                   pl.BlockSpec((N_HEADS, halves, HEAD_DIM, tq), lambda i, j: (0, i, 0, 0))],
        out_shape=[jax.ShapeDtypeStruct((N_HEADS, HEAD_DIM, m), BF16),
                   jax.ShapeDtypeStruct((N_HEADS, nk, tq, HEAD_DIM), BF16),
                   jax.ShapeDtypeStruct((N_HEADS, nk, HEAD_DIM, tq), BF16)],
        compiler_params=_params("parallel", "arbitrary"),
        name="in_proj_qkv",
    )(h, w)


def _cumsum_kernel(x_ref, qc_ref, kc_ref, carry_ref, *, tb):
    @pl.when(pl.program_id(0) == 0)
    def _():
        carry_ref[...] = jnp.zeros_like(carry_ref)

    row = lax.broadcasted_iota(jnp.int32, (tb, tb), 0)
    col = lax.broadcasted_iota(jnp.int32, (tb, tb), 1)
    tri = jnp.where(col <= row, 1.0, 0.0).astype(BF16)
    s = carry_ref[...]
    for limb in _split3(x_ref[...]):
        s = s + jnp.dot(tri, limb, preferred_element_type=F32)
    carry_ref[...] = s[tb - 1:tb, :]
    c2 = s * LOG2E
    limbs_col = jnp.concatenate(_split3(c2), axis=1)
    limbs_row = [limb.astype(F32) for limb in _split3(c2.T)]

    sel_r = lax.broadcasted_iota(jnp.int32, (3 * LANES, HEAD_DIM), 0)
    sel_c = lax.broadcasted_iota(jnp.int32, (3 * LANES, HEAD_DIM), 1)
    lane = lax.broadcasted_iota(jnp.int32, (tb, HEAD_DIM), 1)
    ones_cols = jnp.where(lane < 3, 1.0, 0.0)
    sub = lax.broadcasted_iota(jnp.int32, (HEAD_DIM, tb), 0)
    ones_rows = jnp.where(sub < 3, 0.0, jnp.where(sub < 6, 1.0, 0.0))
    for h in range(N_HEADS):
        target = jnp.where(sel_c < 3, -1, jnp.where(sel_c < 6, (sel_c - 3) * LANES + h, -1))
        pick = jnp.where(sel_r == target, -1.0, 0.0).astype(BF16)
        kc = jnp.dot(limbs_col, pick, preferred_element_type=F32) + ones_cols
        kc_ref[h, 0] = kc.astype(BF16)
        qc = ones_rows
        for l in range(3):
            qc = jnp.where(sub == l, limbs_row[l][h:h + 1, :], qc)
        qc_ref[h] = qc.astype(BF16)


def _cumsum_operands(x, *, tb):
    m, n = x.shape
    return pl.pallas_call(
        functools.partial(_cumsum_kernel, tb=tb),
        grid=(m // tb,),
        in_specs=[pl.BlockSpec((tb, n), lambda i: (i, 0))],
        out_specs=[pl.BlockSpec((N_HEADS, HEAD_DIM, tb), lambda i: (0, 0, i)),
                   pl.BlockSpec((N_HEADS, 1, tb, HEAD_DIM), lambda i: (0, i, 0, 0))],
        out_shape=[jax.ShapeDtypeStruct((N_HEADS, HEAD_DIM, m), BF16),
                   jax.ShapeDtypeStruct((N_HEADS, m // tb, tb, HEAD_DIM), BF16)],
        scratch_shapes=[pltpu.VMEM((1, n), F32)],
        compiler_params=_params("arbitrary"),
        name="forget_cumsum",
    )(x)


def _conv_kernel(u_ref, halo_ref, w_ref, b_ref, lg_ref, lb_ref, o_ref, buf_ref, sh_ref, y_ref, *, ts, rc, rn):
    i = pl.program_id(0)
    span = ts + CONV_HALO - SUBLANES
    off = CONV_HALO - (CONV_K - 1)

    for lb in range(CONV_CH // LANES):
        lanes = slice(lb * LANES, (lb + 1) * LANES)
        halo = halo_ref[:, lanes]
        buf_ref[lb, 0:CONV_HALO, :] = jnp.where(i == 0, jnp.zeros_like(halo), halo)
        buf_ref[lb, CONV_HALO:, :] = u_ref[:, lanes]
        for r in range(1, SUBLANES):
            sh_ref[r - 1, lb] = buf_ref[lb, r:r + span, :]

    for lb in range(CONV_CH // LANES):
        lanes = slice(lb * LANES, (lb + 1) * LANES)

        def chunk(c, carry, lb=lb, lanes=lanes):
            base = pl.multiple_of(c * rc, rc)
            acc = [jnp.zeros((rc // SUBLANES, SUBLANES, LANES), F32) for _ in range(2)]
            for k in range(CONV_K):
                r = (off + k) % SUBLANES
                a = (off + k) - r
                win = (buf_ref[lb, pl.ds(base + a, rc), :] if r == 0
                       else sh_ref[r - 1, lb, pl.ds(base + a, rc), :])
                acc[k % 2] = acc[k % 2] + win.reshape(rc // SUBLANES, SUBLANES, LANES) * w_ref[k, :, lanes][None]
            y_ref[pl.ds(base, rc), lanes] = (acc[0] + acc[1]).reshape(rc, LANES)
            return carry

        lax.fori_loop(0, ts // rc, chunk, 0)

    def norm(c, carry):
        base = pl.multiple_of(c * rn, rn)
        acc = y_ref[pl.ds(base, rn), :] + b_ref[...]
        mu = jnp.mean(acc, axis=-1, keepdims=True)
        d = acc - mu
        var = jnp.mean(d * d, axis=-1, keepdims=True)
        y = d * lax.rsqrt(var + EPS) * lg_ref[...] + lb_ref[...]
        o_ref[pl.ds(base, rn), :] = (y * jax.nn.sigmoid(y)).astype(o_ref.dtype)
        return carry

    lax.fori_loop(0, ts // rn, norm, 0)


def _conformer_conv(u, conv_w, conv_b, ln_g, ln_b, *, ts=512, rc=128, rn=128):
    s, c = u.shape
    halo_blocks = ts // CONV_HALO
    conv_w = jnp.broadcast_to(conv_w[:, None, :], (CONV_K, SUBLANES, c))
    return pl.pallas_call(
        functools.partial(_conv_kernel, ts=ts, rc=rc, rn=rn),
        grid=(s // ts,),
        in_specs=[pl.BlockSpec((ts, c), lambda i: (i, 0)),
                  pl.BlockSpec((CONV_HALO, c), lambda i: (jnp.maximum(i * halo_blocks - 1, 0), 0)),
                  pl.BlockSpec((CONV_K, SUBLANES, c), lambda i: (0, 0, 0)),
                  pl.BlockSpec((1, c), lambda i: (0, 0)),
                  pl.BlockSpec((1, c), lambda i: (0, 0)),
                  pl.BlockSpec((1, c), lambda i: (0, 0))],
        out_specs=pl.BlockSpec((ts, c), lambda i: (i, 0)),
        out_shape=jax.ShapeDtypeStruct((s, c), BF16),
        scratch_shapes=[pltpu.VMEM((c // LANES, ts + CONV_HALO, LANES), F32),
                        pltpu.VMEM((SUBLANES - 1, c // LANES, ts + CONV_HALO - SUBLANES, LANES), F32),
                        pltpu.VMEM((ts, c), F32)],
        compiler_params=_params("parallel"),
        name="conformer_conv",
    )(u, u, conv_w, conv_b, ln_g, ln_b)


def _attn_kernel(qt_ref, qc_ref, k_ref, kc_ref, vt_ref, o_ref, qa_ref, ka_ref, m_ref, l_ref, acc_ref,
                 s0_ref, s1_ref, p0_ref, p1_ref, a0_ref, a1_ref, b0_ref, b1_ref, *, tq, heads):
    qi = pl.program_id(1)
    nk = k_ref.shape[1]
    s_refs, p_refs, a_refs, b_refs = (s0_ref, s1_ref), (p0_ref, p1_ref), (a0_ref, a1_ref), (b0_ref, b1_ref)

    @pl.when(qi == 0)
    def _():
        def widen(j, carry):
            for g in range(heads):
                ka_ref[g, j, :, :HEAD_DIM] = k_ref[g, j]
                ka_ref[g, j, :, HEAD_DIM:] = kc_ref[g, j]
            return carry
        lax.fori_loop(0, nk, widen, 0)

    qa_ref[:, :HEAD_DIM, :] = qt_ref[...]
    qa_ref[:, HEAD_DIM:, :] = qc_ref[...]
    m_ref[...] = jnp.full_like(m_ref, NEG_INF)
    l_ref[...] = jnp.zeros_like(l_ref)
    acc_ref[...] = jnp.zeros_like(acc_ref)
    for slot in range(2):
        p_refs[slot][...] = jnp.zeros_like(p_refs[slot])
        a_refs[slot][...] = jnp.ones_like(a_refs[slot])

    def scores(g, j, slot):
        s = jnp.dot(ka_ref[g, j], qa_ref[g], preferred_element_type=F32)
        s_refs[slot][g] = s
        b_refs[slot][g] = jnp.max(s, axis=0, keepdims=True)

    def weighted_values(g, j, slot):
        acc_ref[g] = a_refs[slot][g] * acc_ref[g] + jnp.dot(vt_ref[g, j], p_refs[slot][g],
                                                            preferred_element_type=F32)

    def softmax(g, slot, diagonal):
        s = s_refs[slot][g]
        if diagonal:
            key = lax.broadcasted_iota(jnp.int32, (tq, tq), 0)
            qry = lax.broadcasted_iota(jnp.int32, (tq, tq), 1)
            s = jnp.where(key <= qry, s, NEG_INF)
            block_max = jnp.max(s, axis=0, keepdims=True)
        else:
            block_max = b_refs[slot][g]
        m_prev = m_ref[g]
        m_new = jnp.maximum(m_prev, block_max)
        alpha = jnp.exp2(m_prev - m_new)
        p = jnp.exp2(s - m_new)
        l_ref[g] = alpha * l_ref[g] + jnp.sum(p, axis=0, keepdims=True)
        m_ref[g] = m_new
        p_refs[slot][g] = p.astype(BF16)
        a_refs[slot][g] = alpha

    def step(j, slot):
        for g in range(heads):
            scores(g, j + 1, 1 - slot)
        for g in range(heads):
            weighted_values(g, jnp.maximum(j - 1, 0), 1 - slot)
        for g in range(heads):
            softmax(g, slot, False)

    def last_step(slot):
        for g in range(heads):
            weighted_values(g, jnp.maximum(qi - 1, 0), 1 - slot)
        for g in range(heads):
            softmax(g, slot, True)
        for g in range(heads):
            weighted_values(g, qi, slot)

    for g in range(heads):
        scores(g, 0, 0)

    def pair(t, carry):
        step(2 * t, 0)
        step(2 * t + 1, 1)
        return carry

    lax.fori_loop(0, qi // 2, pair, 0)
    odd = qi % 2 == 1

    @pl.when(odd)
    def _():
        step(qi - 1, 0)
        last_step(1)

    @pl.when(jnp.logical_not(odd))
    def _():
        last_step(0)

    for g in range(heads):
        o_ref[:, g * HEAD_DIM:(g + 1) * HEAD_DIM] = (acc_ref[g] / l_ref[g]).T.astype(o_ref.dtype)


def _forgetting_attention(q_t, qc, k, kc, v_t, *, tq, heads=2):
    h, _, s = q_t.shape
    nk = s // tq
    q_spec = pl.BlockSpec((heads, HEAD_DIM, tq), lambda hg, qi: (hg, 0, qi))
    k_spec = pl.BlockSpec((heads, nk, tq, HEAD_DIM), lambda hg, qi: (hg, 0, 0, 0))
    return pl.pallas_call(
        functools.partial(_attn_kernel, tq=tq, heads=heads),
        grid=(h // heads, nk),
        in_specs=[q_spec, q_spec, k_spec, k_spec,
                  pl.BlockSpec((heads, nk, HEAD_DIM, tq), lambda hg, qi: (hg, 0, 0, 0))],
        out_specs=pl.BlockSpec((tq, heads * HEAD_DIM), lambda hg, qi: (qi, hg)),
        out_shape=jax.ShapeDtypeStruct((s, h * HEAD_DIM), BF16),
        scratch_shapes=[pltpu.VMEM((heads, MXU_DEPTH, tq), BF16),
                        pltpu.VMEM((heads, nk, tq, MXU_DEPTH), BF16),
                        pltpu.VMEM((heads, 1, tq), F32), pltpu.VMEM((heads, 1, tq), F32),
                        pltpu.VMEM((heads, HEAD_DIM, tq), F32),
                        pltpu.VMEM((heads, tq, tq), F32), pltpu.VMEM((heads, tq, tq), F32),
                        pltpu.VMEM((heads, tq, tq), BF16), pltpu.VMEM((heads, tq, tq), BF16),
                        pltpu.VMEM((heads, 1, tq), F32), pltpu.VMEM((heads, 1, tq), F32),
                        pltpu.VMEM((heads, 1, tq), F32), pltpu.VMEM((heads, 1, tq), F32)],
        compiler_params=_params("parallel", "arbitrary"),
        name="forgetting_attention",
    )(q_t, qc, k, kc, v_t)


def _merge_kernel(x_ref, ac_ref, at_ref, gc_ref, ga_ref, wc_ref, wa_ref, wo_ref, ng_ref, o_ref, h_ref):
    yc = jnp.dot(ac_ref[...], wc_ref[...], preferred_element_type=F32)
    ya = jnp.dot(at_ref[...], wa_ref[...], preferred_element_type=F32)
    merged = gc_ref[...].astype(F32) * yc + ga_ref[...].astype(F32) * ya
    y = x_ref[...] + jnp.dot(merged.astype(BF16), wo_ref[...], preferred_element_type=F32)
    o_ref[...] = y
    h_ref[...] = _rms(y, ng_ref[...]).astype(BF16)


def _merge(x, a_conv, a_attn, gates, w_conv_out, w_attn_out, w_out, ffn_g, *, layer, tm=512):
    s, d = x.shape
    const = lambda i: (layer, 0, 0)
    once = pl.Buffered(1)
    row_spec = pl.BlockSpec((tm, d), lambda i: (i, 0))
    return pl.pallas_call(
        _merge_kernel,
        grid=(s // tm,),
        in_specs=[row_spec,
                  pl.BlockSpec((tm, CONV_CH), lambda i: (i, 0)),
                  pl.BlockSpec((tm, ATTN_W), lambda i: (i, 0)),
                  pl.BlockSpec((tm, d), lambda i: (i, 0)),
                  pl.BlockSpec((tm, d), lambda i: (i, 1)),
                  pl.BlockSpec((None, CONV_CH, d), const, pipeline_mode=once),
                  pl.BlockSpec((None, ATTN_W, d), const, pipeline_mode=once),
                  pl.BlockSpec((None, d, d), const, pipeline_mode=once),
                  pl.BlockSpec((1, d), lambda i: (0, 0))],
        out_specs=[row_spec, row_spec],
        out_shape=[jax.ShapeDtypeStruct((s, d), F32), jax.ShapeDtypeStruct((s, d), BF16)],
        compiler_params=_params("parallel"),
        name="gated_merge",
    )(x, a_conv, a_attn, gates, gates, w_conv_out, w_attn_out, w_out, ffn_g)


def _ffn_kernel(h_ref, wg_ref, wu_ref, wd_ref, o_ref):
    @pl.when(pl.program_id(1) == 0)
    def _():
        o_ref[...] = jnp.zeros_like(o_ref)

    h = h_ref[...]
    gate = jnp.dot(h, wg_ref[...].astype(BF16), preferred_element_type=F32)
    up = jnp.dot(h, wu_ref[...].astype(BF16), preferred_element_type=F32)
    a = (gate * jax.nn.sigmoid(gate) * up).astype(BF16)
    o_ref[...] += jnp.dot(a, wd_ref[...].astype(BF16), preferred_element_type=F32)


def _ffn(h, w_gate_up, w_down, *, layer, tm=1024, th=256):
    s, d = h.shape
    nh = FFN_HIDDEN // th
    return pl.pallas_call(
        _ffn_kernel,
        grid=(s // tm, nh),
        in_specs=[pl.BlockSpec((tm, d), lambda i, c: (i, 0)),
                  pl.BlockSpec((None, d, th), lambda i, c: (layer, 0, c)),
                  pl.BlockSpec((None, d, th), lambda i, c: (layer, 0, c + nh)),
                  pl.BlockSpec((None, th, d), lambda i, c: (layer, c, 0))],
        out_specs=pl.BlockSpec((tm, d), lambda i, c: (i, 0)),
        out_shape=jax.ShapeDtypeStruct((s, d), F32),
        compiler_params=_params("parallel", "arbitrary"),
        name="swiglu_ffn",
    )(h, w_gate_up, w_gate_up, w_down)


def _ple_kernel(x_ref, f_ref, g_ref, p_ref, wg_ref, wp_ref, ng_ref, *o_refs, final):
    x = x_ref[...] + f_ref[...]
    h = _rms(x, g_ref[...]).astype(BF16)
    gate = jax.nn.sigmoid(jnp.dot(h, wg_ref[...], preferred_element_type=F32))
    emb = jnp.dot(p_ref[...].astype(BF16), wp_ref[...], preferred_element_type=F32)
    y = x + gate * emb
    if final:
        o_refs[0][...] = _rms(y, ng_ref[...])
    else:
        o_refs[0][...] = y
        o_refs[1][...] = _rms(y, ng_ref[...]).astype(BF16)


def _ple(x, f, g, p, w_gate, w_proj, next_g, *, layer, final, tm=256):
    s, d = x.shape
    pd = p.shape[-1]
    const = lambda i: (0, 0)
    stacked = lambda i: (layer, 0, 0)
    once = pl.Buffered(1)
    row_spec = pl.BlockSpec((tm, d), lambda i: (i, 0))
    out_specs = [row_spec] if final else [row_spec, row_spec]
    out_shape = [jax.ShapeDtypeStruct((s, d), F32)] + ([] if final else [jax.ShapeDtypeStruct((s, d), BF16)])
    return pl.pallas_call(
        functools.partial(_ple_kernel, final=final),
        grid=(s // tm,),
        in_specs=[row_spec,
                  row_spec,
                  pl.BlockSpec((1, d), const),
                  pl.BlockSpec((None, tm, pd), lambda i: (layer, i, 0)),
                  pl.BlockSpec((None, d, d), stacked, pipeline_mode=once),
                  pl.BlockSpec((None, pd, d), stacked, pipeline_mode=once),
                  pl.BlockSpec((1, d), const)],
        out_specs=out_specs,
        out_shape=out_shape,
        compiler_params=_params("parallel"),
        name="ple_final" if final else "ple",
    )(x, f, g, p, w_gate, w_proj, next_g)


def kernel(x, p, norm_mix_g, w_in, b_forget, conv_w, conv_b, conv_ln_g, conv_ln_b, w_conv_out,
           w_attn_out, w_out, norm_ffn_g, w_gate_up, w_down, norm_ple_g, w_ple_gate, w_ple_proj, final_g):
    b, s, d = x.shape
    depth = w_in.shape[0]
    assert b == 1 and d == D_MODEL
    xs = x.reshape(s, d)
    row = lambda v: v.reshape(1, -1)
    o_f = 2 * CONV_CH + 3 * ATTN_W
    o_g = o_f + N_HEADS
    tq = 512
    w_conv_out, w_attn_out, w_out, w_ple_gate, w_ple_proj = (
        w.astype(BF16) for w in (w_conv_out, w_attn_out, w_out, w_ple_gate, w_ple_proj))
    p = p.reshape(depth, s, -1)
    w_t = jnp.swapaxes(w_in, 1, 2)
    r_glu, r_qkv = 0, 2 * CONV_CH
    h = _rmsnorm_cast(xs, row(norm_mix_g[0]))

    for i in range(depth):
        b_f = jnp.pad(b_forget[i], (0, LANES - N_HEADS)).reshape(1, LANES)

        u = _in_proj(h, w_t, [r_glu, r_glu + CONV_CH], [], _ep_glu, F32,
                     layer=i, n=CONV_CH, tm=1024, tn=512, name="in_proj_glu")
        q_t, k, v_t = _qkv_proj(h, w_t, r_qkv // ATTN_W, layer=i, tm=1024, tq=tq)
        gates = _in_proj(h, w_t, [o_g], [], _ep_sigmoid, BF16,
                         layer=i, n=2 * D_MODEL, tm=1024, tn=1024, name="in_proj_gates")
        log_f = _in_proj(h, w_t, [o_f], [b_f], _ep_log_forget, F32,
                         layer=i, n=LANES, tm=1024, tn=LANES, name="in_proj_forget")

        qc, kc = _cumsum_operands(log_f, tb=tq)
        a_attn = _forgetting_attention(q_t, qc, k, kc, v_t, tq=tq)

        a_conv = _conformer_conv(u, conv_w[i], row(conv_b[i]), row(conv_ln_g[i]), row(conv_ln_b[i]))

        xs, h_ffn = _merge(xs, a_conv, a_attn, gates, w_conv_out, w_attn_out, w_out, row(norm_ffn_g[i]), layer=i)
        f = _ffn(h_ffn, w_gate_up, w_down, layer=i)
        if i == depth - 1:
            (xs,) = _ple(xs, f, row(norm_ple_g[i]), p, w_ple_gate, w_ple_proj, row(final_g), layer=i, final=True)
        else:
            xs, h = _ple(xs, f, row(norm_ple_g[i]), p, w_ple_gate, w_ple_proj, row(norm_mix_g[i + 1]),
                         layer=i, final=False)
    return xs.reshape(b, s, d)
```

```python
import functools
import math

import jax
import jax.numpy as jnp
from jax import lax
from jax.experimental import pallas as pl
from jax.experimental.pallas import tpu as pltpu

D_MODEL = 2048
N_HEADS = 8
HEAD_DIM = 128
ATTN_W = N_HEADS * HEAD_DIM
CONV_CH = D_MODEL // 2
CONV_K = 31
FFN_HIDDEN = 5632
EPS = 1e-6
NEG_INF = -1e30
LOG2E = math.log2(math.e)

LANES = 128
SUBLANES = 8
MXU_DEPTH = 256
CONV_HALO = 32
VMEM_LIMIT_BYTES = 56 * 1024 * 1024

BF16 = jnp.bfloat16
F32 = jnp.float32


def _params(*sem):
    return pltpu.CompilerParams(dimension_semantics=sem, vmem_limit_bytes=VMEM_LIMIT_BYTES)


def _rms(x, g):
    ms = jnp.mean(x * x, axis=-1, keepdims=True)
    return x * lax.rsqrt(ms + EPS) * g


def _dot_nt(a, b_t):
    return lax.dot_general(a, b_t, (((1,), (1,)), ((), ())), preferred_element_type=F32)


def _split3(x):
    hi = x.astype(BF16)
    r1 = x - hi.astype(F32)
    mid = r1.astype(BF16)
    lo = (r1 - mid.astype(F32)).astype(BF16)
    return hi, mid, lo


def _rmsnorm_kernel(x_ref, g_ref, o_ref):
    o_ref[...] = _rms(x_ref[...], g_ref[...]).astype(o_ref.dtype)


def _rmsnorm_cast(x, g, *, tm=512):
    m, k = x.shape
    return pl.pallas_call(
        _rmsnorm_kernel,
        grid=(m // tm,),
        in_specs=[pl.BlockSpec((tm, k), lambda i: (i, 0)), pl.BlockSpec((1, k), lambda i: (0, 0))],
        out_specs=pl.BlockSpec((tm, k), lambda i: (i, 0)),
        out_shape=jax.ShapeDtypeStruct((m, k), BF16),
        compiler_params=_params("parallel"),
        name="mix_norm",
    )(x, g)


def _mm_kernel(*refs, n_w, n_extra, epilogue):
    h_ref = refs[0]
    w_refs = refs[1:1 + n_w]
    e_refs = refs[1 + n_w:1 + n_w + n_extra]
    o_ref = refs[1 + n_w + n_extra]
    h = h_ref[...]
    accs = [_dot_nt(h, w[0].astype(BF16)) for w in w_refs]
    o_ref[...] = epilogue(accs, [e[...] for e in e_refs]).astype(o_ref.dtype)


def _in_proj(h, w_t, row_starts, extras, epilogue, out_dtype, *, layer, n, tm, tn, name):
    m, k = h.shape
    grid = (m // tm, n // tn)
    assert all(r % SUBLANES == 0 for r in row_starts)
    in_specs = [pl.BlockSpec((tm, k), lambda i, j: (i, 0))]
    in_specs += [pl.BlockSpec((pl.Element(1), pl.Element(tn), pl.Element(k)),
                              functools.partial(lambda i, j, r: (layer, pl.multiple_of(r + j * tn, SUBLANES), 0), r=r))
                 for r in row_starts]
    in_specs += [pl.BlockSpec((1, tn), lambda i, j: (0, j)) for _ in extras]
    return pl.pallas_call(
        functools.partial(_mm_kernel, n_w=len(row_starts), n_extra=len(extras), epilogue=epilogue),
        grid=grid,
        in_specs=in_specs,
        out_specs=pl.BlockSpec((tm, tn), lambda i, j: (i, j)),
        out_shape=jax.ShapeDtypeStruct((m, n), out_dtype),
        compiler_params=_params("parallel", "arbitrary"),
        name=name,
    )(h, *([w_t] * len(row_starts)), *extras)


def _ep_sigmoid(accs, extras):
    return jax.nn.sigmoid(accs[0])


def _ep_glu(accs, extras):
    return accs[0] * jax.nn.sigmoid(accs[1])


def _ep_log_forget(accs, extras):
    return jax.nn.log_sigmoid(accs[0] + extras[0])


def _qkv_kernel(h_ref, w_ref, qt_ref, k_ref, vt_ref, *, tq, q_scale):
    j = pl.program_id(1)
    acc = _dot_nt(h_ref[...], w_ref[...].astype(BF16))
    head = lambda h, r: acc[r * tq:(r + 1) * tq, h * HEAD_DIM:(h + 1) * HEAD_DIM]
    halves = k_ref.shape[1]

    @pl.when(j == 0)
    def _():
        for h in range(N_HEADS):
            for r in range(halves):
                qt_ref[h, :, r * tq:(r + 1) * tq] = (head(h, r) * q_scale).T.astype(BF16)

    @pl.when(j == 1)
    def _():
        for h in range(N_HEADS):
            for r in range(halves):
                k_ref[h, r] = head(h, r).astype(BF16)

    @pl.when(j == 2)
    def _():
        for h in range(N_HEADS):
            for r in range(halves):
                vt_ref[h, r] = head(h, r).T.astype(BF16)


def _qkv_proj(h, w, col_tile, *, layer, tm, tq):
    m, k = h.shape
    nk = m // tq
    halves = tm // tq
    q_scale = LOG2E / math.sqrt(HEAD_DIM)
    return pl.pallas_call(
        functools.partial(_qkv_kernel, tq=tq, q_scale=q_scale),
        grid=(m // tm, 3),
        in_specs=[pl.BlockSpec((tm, k), lambda i, j: (i, 0)),
                  pl.BlockSpec((None, ATTN_W, k), lambda i, j: (layer, j + col_tile, 0))],
        out_specs=[pl.BlockSpec((N_HEADS, HEAD_DIM, tm), lambda i, j: (0, 0, i)),
                   pl.BlockSpec((N_HEADS, halves, tq, HEAD_DIM), lambda i, j: (0, i, 0, 0)),
                   pl.BlockSpec((N_HEADS, halves, HEAD_DIM, tq), lambda i, j: (0, i, 0, 0))],
        out_shape=[jax.ShapeDtypeStruct((N_HEADS, HEAD_DIM, m), BF16),
                   jax.ShapeDtypeStruct((N_HEADS, nk, tq, HEAD_DIM), BF16),
                   jax.ShapeDtypeStruct((N_HEADS, nk, HEAD_DIM, tq), BF16)],
        compiler_params=_params("parallel", "arbitrary"),
        name="in_proj_qkv",
    )(h, w)


def _cumsum_kernel(x_ref, qc_ref, kc_ref, carry_ref, *, tb):
    @pl.when(pl.program_id(0) == 0)
    def _():
        carry_ref[...] = jnp.zeros_like(carry_ref)

    row = lax.broadcasted_iota(jnp.int32, (tb, tb), 0)
    col = lax.broadcasted_iota(jnp.int32, (tb, tb), 1)
    tri = jnp.where(col <= row, 1.0, 0.0).astype(BF16)
    s = carry_ref[...]
    for limb in _split3(x_ref[...]):
        s = s + jnp.dot(tri, limb, preferred_element_type=F32)
    carry_ref[...] = s[tb - 1:tb, :]
    c2 = s * LOG2E
    limbs_col = jnp.concatenate(_split3(c2), axis=1)
    limbs_row = [limb.astype(F32) for limb in _split3(c2.T)]

    sel_r = lax.broadcasted_iota(jnp.int32, (3 * LANES, HEAD_DIM), 0)
    sel_c = lax.broadcasted_iota(jnp.int32, (3 * LANES, HEAD_DIM), 1)
    lane = lax.broadcasted_iota(jnp.int32, (tb, HEAD_DIM), 1)
    ones_cols = jnp.where(lane < 3, 1.0, 0.0)
    sub = lax.broadcasted_iota(jnp.int32, (HEAD_DIM, tb), 0)
    ones_rows = jnp.where(sub < 3, 0.0, jnp.where(sub < 6, 1.0, 0.0))
    for h in range(N_HEADS):
        target = jnp.where(sel_c < 3, -1, jnp.where(sel_c < 6, (sel_c - 3) * LANES + h, -1))
        pick = jnp.where(sel_r == target, -1.0, 0.0).astype(BF16)
        kc = jnp.dot(limbs_col, pick, preferred_element_type=F32) + ones_cols
        kc_ref[h, 0] = kc.astype(BF16)
        qc = ones_rows
        for l in range(3):
            qc = jnp.where(sub == l, limbs_row[l][h:h + 1, :], qc)
        qc_ref[h] = qc.astype(BF16)


def _cumsum_operands(x, *, tb):
    m, n = x.shape
    return pl.pallas_call(
        functools.partial(_cumsum_kernel, tb=tb),
        grid=(m // tb,),
        in_specs=[pl.BlockSpec((tb, n), lambda i: (i, 0))],
        out_specs=[pl.BlockSpec((N_HEADS, HEAD_DIM, tb), lambda i: (0, 0, i)),
                   pl.BlockSpec((N_HEADS, 1, tb, HEAD_DIM), lambda i: (0, i, 0, 0))],
        out_shape=[jax.ShapeDtypeStruct((N_HEADS, HEAD_DIM, m), BF16),
                   jax.ShapeDtypeStruct((N_HEADS, m // tb, tb, HEAD_DIM), BF16)],
        scratch_shapes=[pltpu.VMEM((1, n), F32)],
        compiler_params=_params("arbitrary"),
        name="forget_cumsum",
    )(x)


def _conv_kernel(u_ref, halo_ref, w_ref, b_ref, lg_ref, lb_ref, o_ref, buf_ref, sh_ref, y_ref, *, ts, rc, rn):
    i = pl.program_id(0)
    span = ts + CONV_HALO - SUBLANES
    off = CONV_HALO - (CONV_K - 1)

    for lb in range(CONV_CH // LANES):
        lanes = slice(lb * LANES, (lb + 1) * LANES)
        halo = halo_ref[:, lanes]
        buf_ref[lb, 0:CONV_HALO, :] = jnp.where(i == 0, jnp.zeros_like(halo), halo)
        buf_ref[lb, CONV_HALO:, :] = u_ref[:, lanes]
        for r in range(1, SUBLANES):
            sh_ref[r - 1, lb] = buf_ref[lb, r:r + span, :]

    for lb in range(CONV_CH // LANES):
        lanes = slice(lb * LANES, (lb + 1) * LANES)

        def chunk(c, carry, lb=lb, lanes=lanes):
            base = pl.multiple_of(c * rc, rc)
            acc = [jnp.zeros((rc // SUBLANES, SUBLANES, LANES), F32) for _ in range(2)]
            for k in range(CONV_K):
                r = (off + k) % SUBLANES
                a = (off + k) - r
                win = (buf_ref[lb, pl.ds(base + a, rc), :] if r == 0
                       else sh_ref[r - 1, lb, pl.ds(base + a, rc), :])
                acc[k % 2] = acc[k % 2] + win.reshape(rc // SUBLANES, SUBLANES, LANES) * w_ref[k, :, lanes][None]
            y_ref[pl.ds(base, rc), lanes] = (acc[0] + acc[1]).reshape(rc, LANES)
            return carry

        lax.fori_loop(0, ts // rc, chunk, 0)

    def norm(c, carry):
        base = pl.multiple_of(c * rn, rn)
        acc = y_ref[pl.ds(base, rn), :] + b_ref[...]
        mu = jnp.mean(acc, axis=-1, keepdims=True)
        d = acc - mu
        var = jnp.mean(d * d, axis=-1, keepdims=True)
        y = d * lax.rsqrt(var + EPS) * lg_ref[...] + lb_ref[...]
        o_ref[pl.ds(base, rn), :] = (y * jax.nn.sigmoid(y)).astype(o_ref.dtype)
        return carry

    lax.fori_loop(0, ts // rn, norm, 0)


def _conformer_conv(u, conv_w, conv_b, ln_g, ln_b, *, ts=512, rc=128, rn=128):
    s, c = u.shape
    halo_blocks = ts // CONV_HALO
    conv_w = jnp.broadcast_to(conv_w[:, None, :], (CONV_K, SUBLANES, c))
    return pl.pallas_call(
        functools.partial(_conv_kernel, ts=ts, rc=rc, rn=rn),
        grid=(s // ts,),
        in_specs=[pl.BlockSpec((ts, c), lambda i: (i, 0)),
                  pl.BlockSpec((CONV_HALO, c), lambda i: (jnp.maximum(i * halo_blocks - 1, 0), 0)),
                  pl.BlockSpec((CONV_K, SUBLANES, c), lambda i: (0, 0, 0)),
                  pl.BlockSpec((1, c), lambda i: (0, 0)),
                  pl.BlockSpec((1, c), lambda i: (0, 0)),
                  pl.BlockSpec((1, c), lambda i: (0, 0))],
        out_specs=pl.BlockSpec((ts, c), lambda i: (i, 0)),
        out_shape=jax.ShapeDtypeStruct((s, c), BF16),
        scratch_shapes=[pltpu.VMEM((c // LANES, ts + CONV_HALO, LANES), F32),
                        pltpu.VMEM((SUBLANES - 1, c // LANES, ts + CONV_HALO - SUBLANES, LANES), F32),
                        pltpu.VMEM((ts, c), F32)],
        compiler_params=_params("parallel"),
        name="conformer_conv",
    )(u, u, conv_w, conv_b, ln_g, ln_b)


def _attn_kernel(qt_ref, qc_ref, k_ref, kc_ref, vt_ref, o_ref, qa_ref, ka_ref, m_ref, l_ref, acc_ref,
                 s0_ref, s1_ref, p0_ref, p1_ref, a0_ref, a1_ref, b0_ref, b1_ref, *, tq, heads):
    qi = pl.program_id(1)
    nk = k_ref.shape[1]
    s_refs, p_refs, a_refs, b_refs = (s0_ref, s1_ref), (p0_ref, p1_ref), (a0_ref, a1_ref), (b0_ref, b1_ref)

    @pl.when(qi == 0)
    def _():
        def widen(j, carry):
            for g in range(heads):
                ka_ref[g, j, :, :HEAD_DIM] = k_ref[g, j]
                ka_ref[g, j, :, HEAD_DIM:] = kc_ref[g, j]
            return carry
        lax.fori_loop(0, nk, widen, 0)

    qa_ref[:, :HEAD_DIM, :] = qt_ref[...]
    qa_ref[:, HEAD_DIM:, :] = qc_ref[...]
    m_ref[...] = jnp.full_like(m_ref, NEG_INF)
    l_ref[...] = jnp.zeros_like(l_ref)
    acc_ref[...] = jnp.zeros_like(acc_ref)
    for slot in range(2):
        p_refs[slot][...] = jnp.zeros_like(p_refs[slot])
        a_refs[slot][...] = jnp.ones_like(a_refs[slot])

    def scores(g, j, slot):
        s = jnp.dot(ka_ref[g, j], qa_ref[g], preferred_element_type=F32)
        s_refs[slot][g] = s
        b_refs[slot][g] = jnp.max(s, axis=0, keepdims=True)

    def weighted_values(g, j, slot):
        acc_ref[g] = a_refs[slot][g] * acc_ref[g] + jnp.dot(vt_ref[g, j], p_refs[slot][g],
                                                            preferred_element_type=F32)

    def softmax(g, slot, diagonal):
        s = s_refs[slot][g]
        if diagonal:
            key = lax.broadcasted_iota(jnp.int32, (tq, tq), 0)
            qry = lax.broadcasted_iota(jnp.int32, (tq, tq), 1)
            s = jnp.where(key <= qry, s, NEG_INF)
            block_max = jnp.max(s, axis=0, keepdims=True)
        else:
            block_max = b_refs[slot][g]
        m_prev = m_ref[g]
        m_new = jnp.maximum(m_prev, block_max)
        alpha = jnp.exp2(m_prev - m_new)
        p = jnp.exp2(s - m_new)
        l_ref[g] = alpha * l_ref[g] + jnp.sum(p, axis=0, keepdims=True)
        m_ref[g] = m_new
        p_refs[slot][g] = p.astype(BF16)
        a_refs[slot][g] = alpha

    def step(j, slot):
        for g in range(heads):
            scores(g, j + 1, 1 - slot)
        for g in range(heads):
            weighted_values(g, jnp.maximum(j - 1, 0), 1 - slot)
        for g in range(heads):
            softmax(g, slot, False)

    def last_step(slot):
        for g in range(heads):
            weighted_values(g, jnp.maximum(qi - 1, 0), 1 - slot)
        for g in range(heads):
            softmax(g, slot, True)
        for g in range(heads):
            weighted_values(g, qi, slot)

    for g in range(heads):
        scores(g, 0, 0)

    def pair(t, carry):
        step(2 * t, 0)
        step(2 * t + 1, 1)
        return carry

    lax.fori_loop(0, qi // 2, pair, 0)
    odd = qi % 2 == 1

    @pl.when(odd)
    def _():
        step(qi - 1, 0)
        last_step(1)

    @pl.when(jnp.logical_not(odd))
    def _():
        last_step(0)

    for g in range(heads):
        o_ref[:, g * HEAD_DIM:(g + 1) * HEAD_DIM] = (acc_ref[g] / l_ref[g]).T.astype(o_ref.dtype)


def _forgetting_attention(q_t, qc, k, kc, v_t, *, tq, heads=2):
    h, _, s = q_t.shape
    nk = s // tq
    q_spec = pl.BlockSpec((heads, HEAD_DIM, tq), lambda hg, qi: (hg, 0, qi))
    k_spec = pl.BlockSpec((heads, nk, tq, HEAD_DIM), lambda hg, qi: (hg, 0, 0, 0))
    return pl.pallas_call(
        functools.partial(_attn_kernel, tq=tq, heads=heads),
        grid=(h // heads, nk),
        in_specs=[q_spec, q_spec, k_spec, k_spec,
                  pl.BlockSpec((heads, nk, HEAD_DIM, tq), lambda hg, qi: (hg, 0, 0, 0))],
        out_specs=pl.BlockSpec((tq, heads * HEAD_DIM), lambda hg, qi: (qi, hg)),
        out_shape=jax.ShapeDtypeStruct((s, h * HEAD_DIM), BF16),
        scratch_shapes=[pltpu.VMEM((heads, MXU_DEPTH, tq), BF16),
                        pltpu.VMEM((heads, nk, tq, MXU_DEPTH), BF16),
                        pltpu.VMEM((heads, 1, tq), F32), pltpu.VMEM((heads, 1, tq), F32),
                        pltpu.VMEM((heads, HEAD_DIM, tq), F32),
                        pltpu.VMEM((heads, tq, tq), F32), pltpu.VMEM((heads, tq, tq), F32),
                        pltpu.VMEM((heads, tq, tq), BF16), pltpu.VMEM((heads, tq, tq), BF16),
                        pltpu.VMEM((heads, 1, tq), F32), pltpu.VMEM((heads, 1, tq), F32),
                        pltpu.VMEM((heads, 1, tq), F32), pltpu.VMEM((heads, 1, tq), F32)],
        compiler_params=_params("parallel", "arbitrary"),
        name="forgetting_attention",
    )(q_t, qc, k, kc, v_t)


def _merge_kernel(x_ref, ac_ref, at_ref, gc_ref, ga_ref, wc_ref, wa_ref, wo_ref, ng_ref, o_ref, h_ref):
    yc = jnp.dot(ac_ref[...], wc_ref[...], preferred_element_type=F32)
    ya = jnp.dot(at_ref[...], wa_ref[...], preferred_element_type=F32)
    merged = gc_ref[...].astype(F32) * yc + ga_ref[...].astype(F32) * ya
    y = x_ref[...] + jnp.dot(merged.astype(BF16), wo_ref[...], preferred_element_type=F32)
    o_ref[...] = y
    h_ref[...] = _rms(y, ng_ref[...]).astype(BF16)


def _merge(x, a_conv, a_attn, gates, w_conv_out, w_attn_out, w_out, ffn_g, *, layer, tm=512):
    s, d = x.shape
    const = lambda i: (layer, 0, 0)
    once = pl.Buffered(1)
    row_spec = pl.BlockSpec((tm, d), lambda i: (i, 0))
    return pl.pallas_call(
        _merge_kernel,
        grid=(s // tm,),
        in_specs=[row_spec,
                  pl.BlockSpec((tm, CONV_CH), lambda i: (i, 0)),
                  pl.BlockSpec((tm, ATTN_W), lambda i: (i, 0)),
                  pl.BlockSpec((tm, d), lambda i: (i, 0)),
                  pl.BlockSpec((tm, d), lambda i: (i, 1)),
                  pl.BlockSpec((None, CONV_CH, d), const, pipeline_mode=once),
                  pl.BlockSpec((None, ATTN_W, d), const, pipeline_mode=once),
                  pl.BlockSpec((None, d, d), const, pipeline_mode=once),
                  pl.BlockSpec((1, d), lambda i: (0, 0))],
        out_specs=[row_spec, row_spec],
        out_shape=[jax.ShapeDtypeStruct((s, d), F32), jax.ShapeDtypeStruct((s, d), BF16)],
        compiler_params=_params("parallel"),
        name="gated_merge",
    )(x, a_conv, a_attn, gates, gates, w_conv_out, w_attn_out, w_out, ffn_g)


def _ffn_kernel(h_ref, wg_ref, wu_ref, wd_ref, o_ref, *, sub):
    @pl.when(pl.program_id(1) == 0)
    def _():
        o_ref[...] = jnp.zeros_like(o_ref)

    h = h_ref[...]
    for c in range(wd_ref.shape[0] // sub):
        cols = slice(c * sub, (c + 1) * sub)
        gate = jnp.dot(h, wg_ref[:, cols].astype(BF16), preferred_element_type=F32)
        up = jnp.dot(h, wu_ref[:, cols].astype(BF16), preferred_element_type=F32)
        a = (gate * jax.nn.sigmoid(gate) * up).astype(BF16)
        o_ref[...] += jnp.dot(a, wd_ref[cols, :].astype(BF16), preferred_element_type=F32)


def _ffn(h, w_gate_up, w_down, *, layer, tm=1024, th=512, sub=256):
    s, d = h.shape
    nh = FFN_HIDDEN // th
    return pl.pallas_call(
        functools.partial(_ffn_kernel, sub=sub),
        grid=(s // tm, nh),
        in_specs=[pl.BlockSpec((tm, d), lambda i, c: (i, 0)),
                  pl.BlockSpec((None, d, th), lambda i, c: (layer, 0, c)),
                  pl.BlockSpec((None, d, th), lambda i, c: (layer, 0, c + nh)),
                  pl.BlockSpec((None, th, d), lambda i, c: (layer, c, 0))],
        out_specs=pl.BlockSpec((tm, d), lambda i, c: (i, 0)),
        out_shape=jax.ShapeDtypeStruct((s, d), F32),
        compiler_params=_params("parallel", "arbitrary"),
        name="swiglu_ffn",
    )(h, w_gate_up, w_gate_up, w_down)


def _ple_kernel(x_ref, f_ref, g_ref, p_ref, wg_ref, wp_ref, ng_ref, *o_refs, final):
    x = x_ref[...] + f_ref[...]
    h = _rms(x, g_ref[...]).astype(BF16)
    gate = jax.nn.sigmoid(jnp.dot(h, wg_ref[...], preferred_element_type=F32))
    emb = jnp.dot(p_ref[...].astype(BF16), wp_ref[...], preferred_element_type=F32)
    y = x + gate * emb
    if final:
        o_refs[0][...] = _rms(y, ng_ref[...])
    else:
        o_refs[0][...] = y
        o_refs[1][...] = _rms(y, ng_ref[...]).astype(BF16)


def _ple(x, f, g, p, w_gate, w_proj, next_g, *, layer, final, tm=256):
    s, d = x.shape
    pd = p.shape[-1]
    const = lambda i: (0, 0)
    stacked = lambda i: (layer, 0, 0)
    once = pl.Buffered(1)
    row_spec = pl.BlockSpec((tm, d), lambda i: (i, 0))
    out_specs = [row_spec] if final else [row_spec, row_spec]
    out_shape = [jax.ShapeDtypeStruct((s, d), F32)] + ([] if final else [jax.ShapeDtypeStruct((s, d), BF16)])
    return pl.pallas_call(
        functools.partial(_ple_kernel, final=final),
        grid=(s // tm,),
        in_specs=[row_spec,
                  row_spec,
                  pl.BlockSpec((1, d), const),
                  pl.BlockSpec((None, tm, pd), lambda i: (layer, i, 0)),
                  pl.BlockSpec((None, d, d), stacked, pipeline_mode=once),
                  pl.BlockSpec((None, pd, d), stacked, pipeline_mode=once),
                  pl.BlockSpec((1, d), const)],
        out_specs=out_specs,
        out_shape=out_shape,
        compiler_params=_params("parallel"),
        name="ple_final" if final else "ple",
    )(x, f, g, p, w_gate, w_proj, next_g)


def kernel(x, p, norm_mix_g, w_in, b_forget, conv_w, conv_b, conv_ln_g, conv_ln_b, w_conv_out,
           w_attn_out, w_out, norm_ffn_g, w_gate_up, w_down, norm_ple_g, w_ple_gate, w_ple_proj, final_g):
    b, s, d = x.shape
    depth = w_in.shape[0]
    assert b == 1 and d == D_MODEL
    xs = x.reshape(s, d)
    row = lambda v: v.reshape(1, -1)
    o_f = 2 * CONV_CH + 3 * ATTN_W
    o_g = o_f + N_HEADS
    tq = 512
    w_conv_out, w_attn_out, w_out, w_ple_gate, w_ple_proj = (
        w.astype(BF16) for w in (w_conv_out, w_attn_out, w_out, w_ple_gate, w_ple_proj))
    p = p.reshape(depth, s, -1)
    w_t = jnp.swapaxes(w_in, 1, 2)
    r_glu, r_qkv = 0, 2 * CONV_CH
    h = _rmsnorm_cast(xs, row(norm_mix_g[0]))

    for i in range(depth):
        b_f = jnp.pad(b_forget[i], (0, LANES - N_HEADS)).reshape(1, LANES)

        u = _in_proj(h, w_t, [r_glu, r_glu + CONV_CH], [], _ep_glu, F32,
                     layer=i, n=CONV_CH, tm=1024, tn=512, name="in_proj_glu")
        q_t, k, v_t = _qkv_proj(h, w_t, r_qkv // ATTN_W, layer=i, tm=1024, tq=tq)
        gates = _in_proj(h, w_t, [o_g], [], _ep_sigmoid, BF16,
                         layer=i, n=2 * D_MODEL, tm=1024, tn=1024, name="in_proj_gates")
        log_f = _in_proj(h, w_t, [o_f], [b_f], _ep_log_forget, F32,
                         layer=i, n=LANES, tm=1024, tn=LANES, name="in_proj_forget")

        qc, kc = _cumsum_operands(log_f, tb=tq)
        a_attn = _forgetting_attention(q_t, qc, k, kc, v_t, tq=tq)

        a_conv = _conformer_conv(u, conv_w[i], row(conv_b[i]), row(conv_ln_g[i]), row(conv_ln_b[i]))

        xs, h_ffn = _merge(xs, a_conv, a_attn, gates, w_conv_out, w_attn_out, w_out, row(norm_ffn_g[i]), layer=i)
        f = _ffn(h_ffn, w_gate_up, w_down, layer=i)
        if i == depth - 1:
            (xs,) = _ple(xs, f, row(norm_ple_g[i]), p, w_ple_gate, w_ple_proj, row(final_g), layer=i, final=True)
        else:
            xs, h = _ple(xs, f, row(norm_ple_g[i]), p, w_ple_gate, w_ple_proj, row(norm_mix_g[i + 1]),
                         layer=i, final=False)
    return xs.reshape(b, s, d)
```

```python
import functools
import math

import jax
import jax.numpy as jnp
from jax import lax
from jax.experimental import pallas as pl
from jax.experimental.pallas import tpu as pltpu

D_MODEL = 2048
N_HEADS = 8
HEAD_DIM = 128
ATTN_W = N_HEADS * HEAD_DIM
CONV_CH = D_MODEL // 2
CONV_K = 31
FFN_HIDDEN = 5632
EPS = 1e-6
NEG_INF = -1e30
LOG2E = math.log2(math.e)

LANES = 128
SUBLANES = 8
MXU_DEPTH = 256
CONV_HALO = 32
VMEM_LIMIT_BYTES = 56 * 1024 * 1024

BF16 = jnp.bfloat16
F32 = jnp.float32


def _params(*sem):
    return pltpu.CompilerParams(dimension_semantics=sem, vmem_limit_bytes=VMEM_LIMIT_BYTES)


def _rms(x, g):
    ms = jnp.mean(x * x, axis=-1, keepdims=True)
    return x * lax.rsqrt(ms + EPS) * g


def _dot_nt(a, b_t):
    return lax.dot_general(a, b_t, (((1,), (1,)), ((), ())), preferred_element_type=F32)


def _split3(x):
    hi = x.astype(BF16)
    r1 = x - hi.astype(F32)
    mid = r1.astype(BF16)
    lo = (r1 - mid.astype(F32)).astype(BF16)
    return hi, mid, lo


def _rmsnorm_kernel(x_ref, g_ref, o_ref):
    o_ref[...] = _rms(x_ref[...], g_ref[...]).astype(o_ref.dtype)


def _rmsnorm_cast(x, g, *, tm=512):
    m, k = x.shape
    return pl.pallas_call(
        _rmsnorm_kernel,
        grid=(m // tm,),
        in_specs=[pl.BlockSpec((tm, k), lambda i: (i, 0)), pl.BlockSpec((1, k), lambda i: (0, 0))],
        out_specs=pl.BlockSpec((tm, k), lambda i: (i, 0)),
        out_shape=jax.ShapeDtypeStruct((m, k), BF16),
        compiler_params=_params("parallel"),
        name="mix_norm",
    )(x, g)


def _mm_kernel(*refs, n_w, n_extra, epilogue, sub):
    h_ref = refs[0]
    w_refs = refs[1:1 + n_w]
    e_refs = refs[1 + n_w:1 + n_w + n_extra]
    o_ref = refs[1 + n_w + n_extra]
    ws = [w[0].astype(BF16) for w in w_refs]
    extras = [e[...] for e in e_refs]
    for r in range(h_ref.shape[0] // sub):
        rows = slice(r * sub, (r + 1) * sub)
        accs = [_dot_nt(h_ref[rows, :], w) for w in ws]
        o_ref[rows, :] = epilogue(accs, extras).astype(o_ref.dtype)


def _in_proj(h, w_t, row_starts, extras, epilogue, out_dtype, *, layer, n, tm, tn, name, sub=1024):
    m, k = h.shape
    grid = (m // tm, n // tn)
    assert all(r % SUBLANES == 0 for r in row_starts)
    in_specs = [pl.BlockSpec((tm, k), lambda i, j: (i, 0))]
    in_specs += [pl.BlockSpec((pl.Element(1), pl.Element(tn), pl.Element(k)),
                              functools.partial(lambda i, j, r: (layer, pl.multiple_of(r + j * tn, SUBLANES), 0), r=r))
                 for r in row_starts]
    in_specs += [pl.BlockSpec((1, tn), lambda i, j: (0, j)) for _ in extras]
    return pl.pallas_call(
        functools.partial(_mm_kernel, n_w=len(row_starts), n_extra=len(extras), epilogue=epilogue,
                          sub=min(sub, tm)),
        grid=grid,
        in_specs=in_specs,
        out_specs=pl.BlockSpec((tm, tn), lambda i, j: (i, j)),
        out_shape=jax.ShapeDtypeStruct((m, n), out_dtype),
        compiler_params=_params("parallel", "arbitrary"),
        name=name,
    )(h, *([w_t] * len(row_starts)), *extras)


def _ep_sigmoid(accs, extras):
    return jax.nn.sigmoid(accs[0])


def _ep_glu(accs, extras):
    return accs[0] * jax.nn.sigmoid(accs[1])


def _ep_log_forget(accs, extras):
    return jax.nn.log_sigmoid(accs[0] + extras[0])


def _qkv_kernel(h_ref, w_ref, qt_ref, k_ref, vt_ref, *, tq, q_scale):
    j = pl.program_id(1)
    acc = _dot_nt(h_ref[...], w_ref[...].astype(BF16))
    head = lambda h, r: acc[r * tq:(r + 1) * tq, h * HEAD_DIM:(h + 1) * HEAD_DIM]
    halves = k_ref.shape[1]

    @pl.when(j == 0)
    def _():
        for h in range(N_HEADS):
            for r in range(halves):
                qt_ref[h, :, r * tq:(r + 1) * tq] = (head(h, r) * q_scale).T.astype(BF16)

    @pl.when(j == 1)
    def _():
        for h in range(N_HEADS):
            for r in range(halves):
                k_ref[h, r] = head(h, r).astype(BF16)

    @pl.when(j == 2)
    def _():
        for h in range(N_HEADS):
            for r in range(halves):
                vt_ref[h, r] = head(h, r).T.astype(BF16)


def _qkv_proj(h, w, col_tile, *, layer, tm, tq):
    m, k = h.shape
    nk = m // tq
    halves = tm // tq
    q_scale = LOG2E / math.sqrt(HEAD_DIM)
    return pl.pallas_call(
        functools.partial(_qkv_kernel, tq=tq, q_scale=q_scale),
        grid=(m // tm, 3),
        in_specs=[pl.BlockSpec((tm, k), lambda i, j: (i, 0)),
                  pl.BlockSpec((None, ATTN_W, k), lambda i, j: (layer, j + col_tile, 0))],
        out_specs=[pl.BlockSpec((N_HEADS, HEAD_DIM, tm), lambda i, j: (0, 0, i)),
                   pl.BlockSpec((N_HEADS, halves, tq, HEAD_DIM), lambda i, j: (0, i, 0, 0)),
                   pl.BlockSpec((N_HEADS, halves, HEAD_DIM, tq), lambda i, j: (0, i, 0, 0))],
        out_shape=[jax.ShapeDtypeStruct((N_HEADS, HEAD_DIM, m), BF16),
                   jax.ShapeDtypeStruct((N_HEADS, nk, tq, HEAD_DIM), BF16),
                   jax.ShapeDtypeStruct((N_HEADS, nk, HEAD_DIM, tq), BF16)],
        compiler_params=_params("parallel", "arbitrary"),
        name="in_proj_qkv",
    )(h, w)


def _cumsum_kernel(x_ref, qc_ref, kc_ref, carry_ref, *, tb):
    @pl.when(pl.program_id(0) == 0)
    def _():
        carry_ref[...] = jnp.zeros_like(carry_ref)

    row = lax.broadcasted_iota(jnp.int32, (tb, tb), 0)
    col = lax.broadcasted_iota(jnp.int32, (tb, tb), 1)
    tri = jnp.where(col <= row, 1.0, 0.0).astype(BF16)
    s = carry_ref[...]
    for limb in _split3(x_ref[...]):
        s = s + jnp.dot(tri, limb, preferred_element_type=F32)
    carry_ref[...] = s[tb - 1:tb, :]
    c2 = s * LOG2E
    limbs_col = jnp.concatenate(_split3(c2), axis=1)
    limbs_row = [limb.astype(F32) for limb in _split3(c2.T)]

    sel_r = lax.broadcasted_iota(jnp.int32, (3 * LANES, HEAD_DIM), 0)
    sel_c = lax.broadcasted_iota(jnp.int32, (3 * LANES, HEAD_DIM), 1)
    lane = lax.broadcasted_iota(jnp.int32, (tb, HEAD_DIM), 1)
    ones_cols = jnp.where(lane < 3, 1.0, 0.0)
    sub = lax.broadcasted_iota(jnp.int32, (HEAD_DIM, tb), 0)
    ones_rows = jnp.where(sub < 3, 0.0, jnp.where(sub < 6, 1.0, 0.0))
    for h in range(N_HEADS):
        target = jnp.where(sel_c < 3, -1, jnp.where(sel_c < 6, (sel_c - 3) * LANES + h, -1))
        pick = jnp.where(sel_r == target, -1.0, 0.0).astype(BF16)
        kc = jnp.dot(limbs_col, pick, preferred_element_type=F32) + ones_cols
        kc_ref[h, 0] = kc.astype(BF16)
        qc = ones_rows
        for l in range(3):
            qc = jnp.where(sub == l, limbs_row[l][h:h + 1, :], qc)
        qc_ref[h] = qc.astype(BF16)


def _cumsum_operands(x, *, tb):
    m, n = x.shape
    return pl.pallas_call(
        functools.partial(_cumsum_kernel, tb=tb),
        grid=(m // tb,),
        in_specs=[pl.BlockSpec((tb, n), lambda i: (i, 0))],
        out_specs=[pl.BlockSpec((N_HEADS, HEAD_DIM, tb), lambda i: (0, 0, i)),
                   pl.BlockSpec((N_HEADS, 1, tb, HEAD_DIM), lambda i: (0, i, 0, 0))],
        out_shape=[jax.ShapeDtypeStruct((N_HEADS, HEAD_DIM, m), BF16),
                   jax.ShapeDtypeStruct((N_HEADS, m // tb, tb, HEAD_DIM), BF16)],
        scratch_shapes=[pltpu.VMEM((1, n), F32)],
        compiler_params=_params("arbitrary"),
        name="forget_cumsum",
    )(x)


def _conv_kernel(u_ref, halo_ref, w_ref, b_ref, lg_ref, lb_ref, o_ref, buf_ref, sh_ref, y_ref, *, ts, rc, rn):
    i = pl.program_id(0)
    span = ts + CONV_HALO - SUBLANES
    off = CONV_HALO - (CONV_K - 1)

    for lb in range(CONV_CH // LANES):
        lanes = slice(lb * LANES, (lb + 1) * LANES)
        halo = halo_ref[:, lanes]
        buf_ref[lb, 0:CONV_HALO, :] = jnp.where(i == 0, jnp.zeros_like(halo), halo)
        buf_ref[lb, CONV_HALO:, :] = u_ref[:, lanes]
        for r in range(1, SUBLANES):
            sh_ref[r - 1, lb] = buf_ref[lb, r:r + span, :]

    for lb in range(CONV_CH // LANES):
        lanes = slice(lb * LANES, (lb + 1) * LANES)

        def chunk(c, carry, lb=lb, lanes=lanes):
            base = pl.multiple_of(c * rc, rc)
            acc = [jnp.zeros((rc // SUBLANES, SUBLANES, LANES), F32) for _ in range(2)]
            for k in range(CONV_K):
                r = (off + k) % SUBLANES
                a = (off + k) - r
                win = (buf_ref[lb, pl.ds(base + a, rc), :] if r == 0
                       else sh_ref[r - 1, lb, pl.ds(base + a, rc), :])
                acc[k % 2] = acc[k % 2] + win.reshape(rc // SUBLANES, SUBLANES, LANES) * w_ref[k, :, lanes][None]
            y_ref[pl.ds(base, rc), lanes] = (acc[0] + acc[1]).reshape(rc, LANES)
            return carry

        lax.fori_loop(0, ts // rc, chunk, 0)

    def norm(c, carry):
        base = pl.multiple_of(c * rn, rn)
        acc = y_ref[pl.ds(base, rn), :] + b_ref[...]
        mu = jnp.mean(acc, axis=-1, keepdims=True)
        d = acc - mu
        var = jnp.mean(d * d, axis=-1, keepdims=True)
        y = d * lax.rsqrt(var + EPS) * lg_ref[...] + lb_ref[...]
        o_ref[pl.ds(base, rn), :] = (y * jax.nn.sigmoid(y)).astype(o_ref.dtype)
        return carry

    lax.fori_loop(0, ts // rn, norm, 0)


def _conformer_conv(u, conv_w, conv_b, ln_g, ln_b, *, ts=512, rc=128, rn=128):
    s, c = u.shape
    halo_blocks = ts // CONV_HALO
    conv_w = jnp.broadcast_to(conv_w[:, None, :], (CONV_K, SUBLANES, c))
    return pl.pallas_call(
        functools.partial(_conv_kernel, ts=ts, rc=rc, rn=rn),
        grid=(s // ts,),
        in_specs=[pl.BlockSpec((ts, c), lambda i: (i, 0)),
                  pl.BlockSpec((CONV_HALO, c), lambda i: (jnp.maximum(i * halo_blocks - 1, 0), 0)),
                  pl.BlockSpec((CONV_K, SUBLANES, c), lambda i: (0, 0, 0)),
                  pl.BlockSpec((1, c), lambda i: (0, 0)),
                  pl.BlockSpec((1, c), lambda i: (0, 0)),
                  pl.BlockSpec((1, c), lambda i: (0, 0))],
        out_specs=pl.BlockSpec((ts, c), lambda i: (i, 0)),
        out_shape=jax.ShapeDtypeStruct((s, c), BF16),
        scratch_shapes=[pltpu.VMEM((c // LANES, ts + CONV_HALO, LANES), F32),
                        pltpu.VMEM((SUBLANES - 1, c // LANES, ts + CONV_HALO - SUBLANES, LANES), F32),
                        pltpu.VMEM((ts, c), F32)],
        compiler_params=_params("parallel"),
        name="conformer_conv",
    )(u, u, conv_w, conv_b, ln_g, ln_b)


def _attn_kernel(qt_ref, qc_ref, k_ref, kc_ref, vt_ref, o_ref, qa_ref, ka_ref, m_ref, l_ref, acc_ref,
                 s0_ref, s1_ref, p0_ref, p1_ref, a0_ref, a1_ref, b0_ref, b1_ref, *, tq, heads):
    qi = pl.program_id(1)
    nk = k_ref.shape[1]
    s_refs, p_refs, a_refs, b_refs = (s0_ref, s1_ref), (p0_ref, p1_ref), (a0_ref, a1_ref), (b0_ref, b1_ref)

    @pl.when(qi == 0)
    def _():
        def widen(j, carry):
            for g in range(heads):
                ka_ref[g, j, :, :HEAD_DIM] = k_ref[g, j]
                ka_ref[g, j, :, HEAD_DIM:] = kc_ref[g, j]
            return carry
        lax.fori_loop(0, nk, widen, 0)

    qa_ref[:, :HEAD_DIM, :] = qt_ref[...]
    qa_ref[:, HEAD_DIM:, :] = qc_ref[...]
    m_ref[...] = jnp.full_like(m_ref, NEG_INF)
    l_ref[...] = jnp.zeros_like(l_ref)
    acc_ref[...] = jnp.zeros_like(acc_ref)
    for slot in range(2):
        p_refs[slot][...] = jnp.zeros_like(p_refs[slot])
        a_refs[slot][...] = jnp.ones_like(a_refs[slot])

    def scores(g, j, slot):
        s = jnp.dot(ka_ref[g, j], qa_ref[g], preferred_element_type=F32)
        s_refs[slot][g] = s
        b_refs[slot][g] = jnp.max(s, axis=0, keepdims=True)

    def weighted_values(g, j, slot):
        acc_ref[g] = a_refs[slot][g] * acc_ref[g] + jnp.dot(vt_ref[g, j], p_refs[slot][g],
                                                            preferred_element_type=F32)

    def softmax(g, slot, diagonal):
        s = s_refs[slot][g]
        if diagonal:
            key = lax.broadcasted_iota(jnp.int32, (tq, tq), 0)
            qry = lax.broadcasted_iota(jnp.int32, (tq, tq), 1)
            s = jnp.where(key <= qry, s, NEG_INF)
            block_max = jnp.max(s, axis=0, keepdims=True)
        else:
            block_max = b_refs[slot][g]
        m_prev = m_ref[g]
        m_new = jnp.maximum(m_prev, block_max)
        alpha = jnp.exp2(m_prev - m_new)
        p = jnp.exp2(s - m_new)
        l_ref[g] = alpha * l_ref[g] + jnp.sum(p, axis=0, keepdims=True)
        m_ref[g] = m_new
        p_refs[slot][g] = p.astype(BF16)
        a_refs[slot][g] = alpha

    def step(j, slot):
        for g in range(heads):
            scores(g, j + 1, 1 - slot)
        for g in range(heads):
            weighted_values(g, jnp.maximum(j - 1, 0), 1 - slot)
        for g in range(heads):
            softmax(g, slot, False)

    def last_step(slot):
        for g in range(heads):
            weighted_values(g, jnp.maximum(qi - 1, 0), 1 - slot)
        for g in range(heads):
            softmax(g, slot, True)
        for g in range(heads):
            weighted_values(g, qi, slot)

    for g in range(heads):
        scores(g, 0, 0)

    def pair(t, carry):
        step(2 * t, 0)
        step(2 * t + 1, 1)
        return carry

    lax.fori_loop(0, qi // 2, pair, 0)
    odd = qi % 2 == 1

    @pl.when(odd)
    def _():
        step(qi - 1, 0)
        last_step(1)

    @pl.when(jnp.logical_not(odd))
    def _():
        last_step(0)

    for g in range(heads):
        o_ref[:, g * HEAD_DIM:(g + 1) * HEAD_DIM] = (acc_ref[g] / l_ref[g]).T.astype(o_ref.dtype)


def _forgetting_attention(q_t, qc, k, kc, v_t, *, tq, heads=2):
    h, _, s = q_t.shape
    nk = s // tq
    q_spec = pl.BlockSpec((heads, HEAD_DIM, tq), lambda hg, qi: (hg, 0, qi))
    k_spec = pl.BlockSpec((heads, nk, tq, HEAD_DIM), lambda hg, qi: (hg, 0, 0, 0))
    return pl.pallas_call(
        functools.partial(_attn_kernel, tq=tq, heads=heads),
        grid=(h // heads, nk),
        in_specs=[q_spec, q_spec, k_spec, k_spec,
                  pl.BlockSpec((heads, nk, HEAD_DIM, tq), lambda hg, qi: (hg, 0, 0, 0))],
        out_specs=pl.BlockSpec((tq, heads * HEAD_DIM), lambda hg, qi: (qi, hg)),
        out_shape=jax.ShapeDtypeStruct((s, h * HEAD_DIM), BF16),
        scratch_shapes=[pltpu.VMEM((heads, MXU_DEPTH, tq), BF16),
                        pltpu.VMEM((heads, nk, tq, MXU_DEPTH), BF16),
                        pltpu.VMEM((heads, 1, tq), F32), pltpu.VMEM((heads, 1, tq), F32),
                        pltpu.VMEM((heads, HEAD_DIM, tq), F32),
                        pltpu.VMEM((heads, tq, tq), F32), pltpu.VMEM((heads, tq, tq), F32),
                        pltpu.VMEM((heads, tq, tq), BF16), pltpu.VMEM((heads, tq, tq), BF16),
                        pltpu.VMEM((heads, 1, tq), F32), pltpu.VMEM((heads, 1, tq), F32),
                        pltpu.VMEM((heads, 1, tq), F32), pltpu.VMEM((heads, 1, tq), F32)],
        compiler_params=_params("parallel", "arbitrary"),
        name="forgetting_attention",
    )(q_t, qc, k, kc, v_t)


def _merge_kernel(x_ref, ac_ref, at_ref, gc_ref, ga_ref, wc_ref, wa_ref, wo_ref, ng_ref, o_ref, h_ref):
    yc = jnp.dot(ac_ref[...], wc_ref[...], preferred_element_type=F32)
    ya = jnp.dot(at_ref[...], wa_ref[...], preferred_element_type=F32)
    merged = gc_ref[...].astype(F32) * yc + ga_ref[...].astype(F32) * ya
    y = x_ref[...] + jnp.dot(merged.astype(BF16), wo_ref[...], preferred_element_type=F32)
    o_ref[...] = y
    h_ref[...] = _rms(y, ng_ref[...]).astype(BF16)


def _merge(x, a_conv, a_attn, gates, w_conv_out, w_attn_out, w_out, ffn_g, *, layer, tm=512):
    s, d = x.shape
    const = lambda i: (layer, 0, 0)
    once = pl.Buffered(1)
    row_spec = pl.BlockSpec((tm, d), lambda i: (i, 0))
    return pl.pallas_call(
        _merge_kernel,
        grid=(s // tm,),
        in_specs=[row_spec,
                  pl.BlockSpec((tm, CONV_CH), lambda i: (i, 0)),
                  pl.BlockSpec((tm, ATTN_W), lambda i: (i, 0)),
                  pl.BlockSpec((tm, d), lambda i: (i, 0)),
                  pl.BlockSpec((tm, d), lambda i: (i, 1)),
                  pl.BlockSpec((None, CONV_CH, d), const, pipeline_mode=once),
                  pl.BlockSpec((None, ATTN_W, d), const, pipeline_mode=once),
                  pl.BlockSpec((None, d, d), const, pipeline_mode=once),
                  pl.BlockSpec((1, d), lambda i: (0, 0))],
        out_specs=[row_spec, row_spec],
        out_shape=[jax.ShapeDtypeStruct((s, d), F32), jax.ShapeDtypeStruct((s, d), BF16)],
        compiler_params=_params("parallel"),
        name="gated_merge",
    )(x, a_conv, a_attn, gates, gates, w_conv_out, w_attn_out, w_out, ffn_g)


def _ffn_kernel(h_ref, wg_ref, wu_ref, wd_ref, o_ref, *, sub):
    @pl.when(pl.program_id(1) == 0)
    def _():
        o_ref[...] = jnp.zeros_like(o_ref)

    h = h_ref[...]
    for c in range(wd_ref.shape[0] // sub):
        cols = slice(c * sub, (c + 1) * sub)
        gate = jnp.dot(h, wg_ref[:, cols].astype(BF16), preferred_element_type=F32)
        up = jnp.dot(h, wu_ref[:, cols].astype(BF16), preferred_element_type=F32)
        a = (gate * jax.nn.sigmoid(gate) * up).astype(BF16)
        o_ref[...] += jnp.dot(a, wd_ref[cols, :].astype(BF16), preferred_element_type=F32)


def _ffn(h, w_gate_up, w_down, *, layer, tm=1024, th=512, sub=256):
    s, d = h.shape
    nh = FFN_HIDDEN // th
    return pl.pallas_call(
        functools.partial(_ffn_kernel, sub=sub),
        grid=(s // tm, nh),
        in_specs=[pl.BlockSpec((tm, d), lambda i, c: (i, 0)),
                  pl.BlockSpec((None, d, th), lambda i, c: (layer, 0, c)),
                  pl.BlockSpec((None, d, th), lambda i, c: (layer, 0, c + nh)),
                  pl.BlockSpec((None, th, d), lambda i, c: (layer, c, 0))],
        out_specs=pl.BlockSpec((tm, d), lambda i, c: (i, 0)),
        out_shape=jax.ShapeDtypeStruct((s, d), F32),
        compiler_params=_params("parallel", "arbitrary"),
        name="swiglu_ffn",
    )(h, w_gate_up, w_gate_up, w_down)


def _ple_kernel(x_ref, f_ref, g_ref, p_ref, wg_ref, wp_ref, ng_ref, *o_refs, final, sub):
    for r in range(x_ref.shape[0] // sub):
        rows = slice(r * sub, (r + 1) * sub)
        x = x_ref[rows, :] + f_ref[rows, :]
        h = _rms(x, g_ref[...]).astype(BF16)
        gate = jax.nn.sigmoid(jnp.dot(h, wg_ref[...], preferred_element_type=F32))
        emb = jnp.dot(p_ref[rows, :].astype(BF16), wp_ref[...], preferred_element_type=F32)
        y = x + gate * emb
        if final:
            o_refs[0][rows, :] = _rms(y, ng_ref[...])
        else:
            o_refs[0][rows, :] = y
            o_refs[1][rows, :] = _rms(y, ng_ref[...]).astype(BF16)


def _ple(x, f, g, p, w_gate, w_proj, next_g, *, layer, final, tm=512, sub=256):
    s, d = x.shape
    pd = p.shape[-1]
    const = lambda i: (0, 0)
    stacked = lambda i: (layer, 0, 0)
    once = pl.Buffered(1)
    row_spec = pl.BlockSpec((tm, d), lambda i: (i, 0))
    out_specs = [row_spec] if final else [row_spec, row_spec]
    out_shape = [jax.ShapeDtypeStruct((s, d), F32)] + ([] if final else [jax.ShapeDtypeStruct((s, d), BF16)])
    return pl.pallas_call(
        functools.partial(_ple_kernel, final=final, sub=sub),
        grid=(s // tm,),
        in_specs=[row_spec,
                  row_spec,
                  pl.BlockSpec((1, d), const),
                  pl.BlockSpec((None, tm, pd), lambda i: (layer, i, 0)),
                  pl.BlockSpec((None, d, d), stacked, pipeline_mode=once),
                  pl.BlockSpec((None, pd, d), stacked, pipeline_mode=once),
                  pl.BlockSpec((1, d), const)],
        out_specs=out_specs,
        out_shape=out_shape,
        compiler_params=_params("parallel"),
        name="ple_final" if final else "ple",
    )(x, f, g, p, w_gate, w_proj, next_g)


def kernel(x, p, norm_mix_g, w_in, b_forget, conv_w, conv_b, conv_ln_g, conv_ln_b, w_conv_out,
           w_attn_out, w_out, norm_ffn_g, w_gate_up, w_down, norm_ple_g, w_ple_gate, w_ple_proj, final_g):
    b, s, d = x.shape
    depth = w_in.shape[0]
    assert b == 1 and d == D_MODEL
    xs = x.reshape(s, d)
    row = lambda v: v.reshape(1, -1)
    o_f = 2 * CONV_CH + 3 * ATTN_W
    o_g = o_f + N_HEADS
    tq = 512
    w_conv_out, w_attn_out, w_out, w_ple_gate, w_ple_proj = (
        w.astype(BF16) for w in (w_conv_out, w_attn_out, w_out, w_ple_gate, w_ple_proj))
    p = p.reshape(depth, s, -1)
    w_t = jnp.swapaxes(w_in, 1, 2)
    r_glu, r_qkv = 0, 2 * CONV_CH
    h = _rmsnorm_cast(xs, row(norm_mix_g[0]))

    for i in range(depth):
        b_f = jnp.pad(b_forget[i], (0, LANES - N_HEADS)).reshape(1, LANES)

        u = _in_proj(h, w_t, [r_glu, r_glu + CONV_CH], [], _ep_glu, F32,
                     layer=i, n=CONV_CH, tm=2048, tn=512, name="in_proj_glu")
        q_t, k, v_t = _qkv_proj(h, w_t, r_qkv // ATTN_W, layer=i, tm=1024, tq=tq)
        gates = _in_proj(h, w_t, [o_g], [], _ep_sigmoid, BF16,
                         layer=i, n=2 * D_MODEL, tm=2048, tn=1024, name="in_proj_gates")
        log_f = _in_proj(h, w_t, [o_f], [b_f], _ep_log_forget, F32,
                         layer=i, n=LANES, tm=1024, tn=LANES, name="in_proj_forget")

        qc, kc = _cumsum_operands(log_f, tb=tq)
        a_attn = _forgetting_attention(q_t, qc, k, kc, v_t, tq=tq)

        a_conv = _conformer_conv(u, conv_w[i], row(conv_b[i]), row(conv_ln_g[i]), row(conv_ln_b[i]))

        xs, h_ffn = _merge(xs, a_conv, a_attn, gates, w_conv_out, w_attn_out, w_out, row(norm_ffn_g[i]), layer=i)
        f = _ffn(h_ffn, w_gate_up, w_down, layer=i)
        if i == depth - 1:
            (xs,) = _ple(xs, f, row(norm_ple_g[i]), p, w_ple_gate, w_ple_proj, row(final_g), layer=i, final=True)
        else:
            xs, h = _ple(xs, f, row(norm_ple_g[i]), p, w_ple_gate, w_ple_proj, row(norm_mix_g[i + 1]),
                         layer=i, final=False)
    return xs.reshape(b, s, d)
```

```python
import functools
import math

import jax
import jax.numpy as jnp
from jax import lax
from jax.experimental import pallas as pl
from jax.experimental.pallas import tpu as pltpu

D_MODEL = 2048
N_HEADS = 8
HEAD_DIM = 128
ATTN_W = N_HEADS * HEAD_DIM
CONV_CH = D_MODEL // 2
CONV_K = 31
FFN_HIDDEN = 5632
EPS = 1e-6
NEG_INF = -1e30
LOG2E = math.log2(math.e)

LANES = 128
SUBLANES = 8
MXU_DEPTH = 256
CONV_HALO = 32
VMEM_LIMIT_BYTES = 56 * 1024 * 1024

BF16 = jnp.bfloat16
F32 = jnp.float32


def _params(*sem):
    return pltpu.CompilerParams(dimension_semantics=sem, vmem_limit_bytes=VMEM_LIMIT_BYTES)


def _rms(x, g):
    ms = jnp.mean(x * x, axis=-1, keepdims=True)
    return x * lax.rsqrt(ms + EPS) * g


def _dot_nt(a, b_t):
    return lax.dot_general(a, b_t, (((1,), (1,)), ((), ())), preferred_element_type=F32)


def _split3(x):
    hi = x.astype(BF16)
    r1 = x - hi.astype(F32)
    mid = r1.astype(BF16)
    lo = (r1 - mid.astype(F32)).astype(BF16)
    return hi, mid, lo


def _rmsnorm_kernel(x_ref, g_ref, o_ref):
    o_ref[...] = _rms(x_ref[...], g_ref[...]).astype(o_ref.dtype)


def _rmsnorm_cast(x, g, *, tm=512):
    m, k = x.shape
    return pl.pallas_call(
        _rmsnorm_kernel,
        grid=(m // tm,),
        in_specs=[pl.BlockSpec((tm, k), lambda i: (i, 0)), pl.BlockSpec((1, k), lambda i: (0, 0))],
        out_specs=pl.BlockSpec((tm, k), lambda i: (i, 0)),
        out_shape=jax.ShapeDtypeStruct((m, k), BF16),
        compiler_params=_params("parallel"),
        name="mix_norm",
    )(x, g)


def _mm_kernel(*refs, n_w, n_extra, epilogue, sub):
    h_ref = refs[0]
    w_refs = refs[1:1 + n_w]
    e_refs = refs[1 + n_w:1 + n_w + n_extra]
    o_ref = refs[1 + n_w + n_extra]
    ws = [w[0].astype(BF16) for w in w_refs]
    extras = [e[...] for e in e_refs]
    for r in range(h_ref.shape[0] // sub):
        rows = slice(r * sub, (r + 1) * sub)
        accs = [_dot_nt(h_ref[rows, :], w) for w in ws]
        o_ref[rows, :] = epilogue(accs, extras).astype(o_ref.dtype)


def _in_proj(h, w_t, row_starts, extras, epilogue, out_dtype, *, layer, n, tm, tn, name, sub=1024):
    m, k = h.shape
    grid = (m // tm, n // tn)
    assert all(r % SUBLANES == 0 for r in row_starts)
    in_specs = [pl.BlockSpec((tm, k), lambda i, j: (i, 0))]
    in_specs += [pl.BlockSpec((pl.Element(1), pl.Element(tn), pl.Element(k)),
                              functools.partial(lambda i, j, r: (layer, pl.multiple_of(r + j * tn, SUBLANES), 0), r=r))
                 for r in row_starts]
    in_specs += [pl.BlockSpec((1, tn), lambda i, j: (0, j)) for _ in extras]
    return pl.pallas_call(
        functools.partial(_mm_kernel, n_w=len(row_starts), n_extra=len(extras), epilogue=epilogue,
                          sub=min(sub, tm)),
        grid=grid,
        in_specs=in_specs,
        out_specs=pl.BlockSpec((tm, tn), lambda i, j: (i, j)),
        out_shape=jax.ShapeDtypeStruct((m, n), out_dtype),
        compiler_params=_params("parallel", "arbitrary"),
        name=name,
    )(h, *([w_t] * len(row_starts)), *extras)


def _ep_sigmoid(accs, extras):
    return jax.nn.sigmoid(accs[0])


def _ep_glu(accs, extras):
    return accs[0] * jax.nn.sigmoid(accs[1])


def _ep_log_forget(accs, extras):
    return jax.nn.log_sigmoid(accs[0] + extras[0])


def _qkv_kernel(h_ref, w_ref, qt_ref, k_ref, vt_ref, *, tq, q_scale):
    j = pl.program_id(1)
    halves = k_ref.shape[1]

    def project(store):
        w = w_ref[...].astype(BF16)
        for r in range(halves):
            acc = _dot_nt(h_ref[r * tq:(r + 1) * tq, :], w)
            for h in range(N_HEADS):
                store(h, r, acc[:, h * HEAD_DIM:(h + 1) * HEAD_DIM])

    @pl.when(j == 0)
    def _():
        def store(h, r, t):
            qt_ref[h, :, r * tq:(r + 1) * tq] = (t * q_scale).T.astype(BF16)
        project(store)

    @pl.when(j == 1)
    def _():
        def store(h, r, t):
            k_ref[h, r] = t.astype(BF16)
        project(store)

    @pl.when(j == 2)
    def _():
        def store(h, r, t):
            vt_ref[h, r] = t.T.astype(BF16)
        project(store)


def _qkv_proj(h, w, col_tile, *, layer, tm, tq):
    m, k = h.shape
    nk = m // tq
    halves = tm // tq
    q_scale = LOG2E / math.sqrt(HEAD_DIM)
    return pl.pallas_call(
        functools.partial(_qkv_kernel, tq=tq, q_scale=q_scale),
        grid=(m // tm, 3),
        in_specs=[pl.BlockSpec((tm, k), lambda i, j: (i, 0)),
                  pl.BlockSpec((None, ATTN_W, k), lambda i, j: (layer, j + col_tile, 0))],
        out_specs=[pl.BlockSpec((N_HEADS, HEAD_DIM, tm), lambda i, j: (0, 0, i)),
                   pl.BlockSpec((N_HEADS, halves, tq, HEAD_DIM), lambda i, j: (0, i, 0, 0)),
                   pl.BlockSpec((N_HEADS, halves, HEAD_DIM, tq), lambda i, j: (0, i, 0, 0))],
        out_shape=[jax.ShapeDtypeStruct((N_HEADS, HEAD_DIM, m), BF16),
                   jax.ShapeDtypeStruct((N_HEADS, nk, tq, HEAD_DIM), BF16),
                   jax.ShapeDtypeStruct((N_HEADS, nk, HEAD_DIM, tq), BF16)],
        compiler_params=_params("parallel", "arbitrary"),
        name="in_proj_qkv",
    )(h, w)


def _cumsum_kernel(x_ref, qc_ref, kc_ref, carry_ref, *, tb):
    @pl.when(pl.program_id(0) == 0)
    def _():
        carry_ref[...] = jnp.zeros_like(carry_ref)

    row = lax.broadcasted_iota(jnp.int32, (tb, tb), 0)
    col = lax.broadcasted_iota(jnp.int32, (tb, tb), 1)
    tri = jnp.where(col <= row, 1.0, 0.0).astype(BF16)
    s = carry_ref[...]
    for limb in _split3(x_ref[...]):
        s = s + jnp.dot(tri, limb, preferred_element_type=F32)
    carry_ref[...] = s[tb - 1:tb, :]
    c2 = s * LOG2E
    limbs_col = jnp.concatenate(_split3(c2), axis=1)
    limbs_row = [limb.astype(F32) for limb in _split3(c2.T)]

    sel_r = lax.broadcasted_iota(jnp.int32, (3 * LANES, HEAD_DIM), 0)
    sel_c = lax.broadcasted_iota(jnp.int32, (3 * LANES, HEAD_DIM), 1)
    lane = lax.broadcasted_iota(jnp.int32, (tb, HEAD_DIM), 1)
    ones_cols = jnp.where(lane < 3, 1.0, 0.0)
    sub = lax.broadcasted_iota(jnp.int32, (HEAD_DIM, tb), 0)
    ones_rows = jnp.where(sub < 3, 0.0, jnp.where(sub < 6, 1.0, 0.0))
    for h in range(N_HEADS):
        target = jnp.where(sel_c < 3, -1, jnp.where(sel_c < 6, (sel_c - 3) * LANES + h, -1))
        pick = jnp.where(sel_r == target, -1.0, 0.0).astype(BF16)
        kc = jnp.dot(limbs_col, pick, preferred_element_type=F32) + ones_cols
        kc_ref[h, 0] = kc.astype(BF16)
        qc = ones_rows
        for l in range(3):
            qc = jnp.where(sub == l, limbs_row[l][h:h + 1, :], qc)
        qc_ref[h] = qc.astype(BF16)


def _cumsum_operands(x, *, tb):
    m, n = x.shape
    return pl.pallas_call(
        functools.partial(_cumsum_kernel, tb=tb),
        grid=(m // tb,),
        in_specs=[pl.BlockSpec((tb, n), lambda i: (i, 0))],
        out_specs=[pl.BlockSpec((N_HEADS, HEAD_DIM, tb), lambda i: (0, 0, i)),
                   pl.BlockSpec((N_HEADS, 1, tb, HEAD_DIM), lambda i: (0, i, 0, 0))],
        out_shape=[jax.ShapeDtypeStruct((N_HEADS, HEAD_DIM, m), BF16),
                   jax.ShapeDtypeStruct((N_HEADS, m // tb, tb, HEAD_DIM), BF16)],
        scratch_shapes=[pltpu.VMEM((1, n), F32)],
        compiler_params=_params("arbitrary"),
        name="forget_cumsum",
    )(x)


def _conv_kernel(u_ref, halo_ref, w_ref, b_ref, lg_ref, lb_ref, o_ref, buf_ref, sh_ref, y_ref, *, ts, rc, rn):
    i = pl.program_id(0)
    span = ts + CONV_HALO - SUBLANES
    off = CONV_HALO - (CONV_K - 1)

    for lb in range(CONV_CH // LANES):
        lanes = slice(lb * LANES, (lb + 1) * LANES)
        halo = halo_ref[:, lanes]
        buf_ref[lb, 0:CONV_HALO, :] = jnp.where(i == 0, jnp.zeros_like(halo), halo)
        buf_ref[lb, CONV_HALO:, :] = u_ref[:, lanes]
        for r in range(1, SUBLANES):
            sh_ref[r - 1, lb] = buf_ref[lb, r:r + span, :]

    for lb in range(CONV_CH // LANES):
        lanes = slice(lb * LANES, (lb + 1) * LANES)

        def chunk(c, carry, lb=lb, lanes=lanes):
            base = pl.multiple_of(c * rc, rc)
            acc = [jnp.zeros((rc // SUBLANES, SUBLANES, LANES), F32) for _ in range(2)]
            for k in range(CONV_K):
                r = (off + k) % SUBLANES
                a = (off + k) - r
                win = (buf_ref[lb, pl.ds(base + a, rc), :] if r == 0
                       else sh_ref[r - 1, lb, pl.ds(base + a, rc), :])
                acc[k % 2] = acc[k % 2] + win.reshape(rc // SUBLANES, SUBLANES, LANES) * w_ref[k, :, lanes][None]
            y_ref[pl.ds(base, rc), lanes] = (acc[0] + acc[1]).reshape(rc, LANES)
            return carry

        lax.fori_loop(0, ts // rc, chunk, 0)

    def norm(c, carry):
        base = pl.multiple_of(c * rn, rn)
        acc = y_ref[pl.ds(base, rn), :] + b_ref[...]
        mu = jnp.mean(acc, axis=-1, keepdims=True)
        d = acc - mu
        var = jnp.mean(d * d, axis=-1, keepdims=True)
        y = d * lax.rsqrt(var + EPS) * lg_ref[...] + lb_ref[...]
        o_ref[pl.ds(base, rn), :] = (y * jax.nn.sigmoid(y)).astype(o_ref.dtype)
        return carry

    lax.fori_loop(0, ts // rn, norm, 0)


def _conformer_conv(u, conv_w, conv_b, ln_g, ln_b, *, ts=512, rc=128, rn=128):
    s, c = u.shape
    halo_blocks = ts // CONV_HALO
    conv_w = jnp.broadcast_to(conv_w[:, None, :], (CONV_K, SUBLANES, c))
    return pl.pallas_call(
        functools.partial(_conv_kernel, ts=ts, rc=rc, rn=rn),
        grid=(s // ts,),
        in_specs=[pl.BlockSpec((ts, c), lambda i: (i, 0)),
                  pl.BlockSpec((CONV_HALO, c), lambda i: (jnp.maximum(i * halo_blocks - 1, 0), 0)),
                  pl.BlockSpec((CONV_K, SUBLANES, c), lambda i: (0, 0, 0)),
                  pl.BlockSpec((1, c), lambda i: (0, 0)),
                  pl.BlockSpec((1, c), lambda i: (0, 0)),
                  pl.BlockSpec((1, c), lambda i: (0, 0))],
        out_specs=pl.BlockSpec((ts, c), lambda i: (i, 0)),
        out_shape=jax.ShapeDtypeStruct((s, c), BF16),
        scratch_shapes=[pltpu.VMEM((c // LANES, ts + CONV_HALO, LANES), F32),
                        pltpu.VMEM((SUBLANES - 1, c // LANES, ts + CONV_HALO - SUBLANES, LANES), F32),
                        pltpu.VMEM((ts, c), F32)],
        compiler_params=_params("parallel"),
        name="conformer_conv",
    )(u, u, conv_w, conv_b, ln_g, ln_b)


def _attn_kernel(qt_ref, qc_ref, k_ref, kc_ref, vt_ref, o_ref, qa_ref, ka_ref, m_ref, l_ref, acc_ref,
                 s0_ref, s1_ref, p0_ref, p1_ref, a0_ref, a1_ref, b0_ref, b1_ref, *, tq, heads):
    qi = pl.program_id(1)
    nk = k_ref.shape[1]
    s_refs, p_refs, a_refs, b_refs = (s0_ref, s1_ref), (p0_ref, p1_ref), (a0_ref, a1_ref), (b0_ref, b1_ref)

    @pl.when(qi == 0)
    def _():
        def widen(j, carry):
            for g in range(heads):
                ka_ref[g, j, :, :HEAD_DIM] = k_ref[g, j]
                ka_ref[g, j, :, HEAD_DIM:] = kc_ref[g, j]
            return carry
        lax.fori_loop(0, nk, widen, 0)

    qa_ref[:, :HEAD_DIM, :] = qt_ref[...]
    qa_ref[:, HEAD_DIM:, :] = qc_ref[...]
    m_ref[...] = jnp.full_like(m_ref, NEG_INF)
    l_ref[...] = jnp.zeros_like(l_ref)
    acc_ref[...] = jnp.zeros_like(acc_ref)
    for slot in range(2):
        p_refs[slot][...] = jnp.zeros_like(p_refs[slot])
        a_refs[slot][...] = jnp.ones_like(a_refs[slot])

    def scores(g, j, slot):
        s = jnp.dot(ka_ref[g, j], qa_ref[g], preferred_element_type=F32)
        s_refs[slot][g] = s
        b_refs[slot][g] = jnp.max(s, axis=0, keepdims=True)

    def weighted_values(g, j, slot):
        acc_ref[g] = a_refs[slot][g] * acc_ref[g] + jnp.dot(vt_ref[g, j], p_refs[slot][g],
                                                            preferred_element_type=F32)

    def softmax(g, slot, diagonal):
        s = s_refs[slot][g]
        if diagonal:
            key = lax.broadcasted_iota(jnp.int32, (tq, tq), 0)
            qry = lax.broadcasted_iota(jnp.int32, (tq, tq), 1)
            s = jnp.where(key <= qry, s, NEG_INF)
            block_max = jnp.max(s, axis=0, keepdims=True)
        else:
            block_max = b_refs[slot][g]
        m_prev = m_ref[g]
        m_new = jnp.maximum(m_prev, block_max)
        alpha = jnp.exp2(m_prev - m_new)
        p = jnp.exp2(s - m_new)
        l_ref[g] = alpha * l_ref[g] + jnp.sum(p, axis=0, keepdims=True)
        m_ref[g] = m_new
        p_refs[slot][g] = p.astype(BF16)
        a_refs[slot][g] = alpha

    def step(j, slot):
        for g in range(heads):
            scores(g, j + 1, 1 - slot)
        for g in range(heads):
            weighted_values(g, jnp.maximum(j - 1, 0), 1 - slot)
        for g in range(heads):
            softmax(g, slot, False)

    def last_step(slot):
        for g in range(heads):
            weighted_values(g, jnp.maximum(qi - 1, 0), 1 - slot)
        for g in range(heads):
            softmax(g, slot, True)
        for g in range(heads):
            weighted_values(g, qi, slot)

    for g in range(heads):
        scores(g, 0, 0)

    def pair(t, carry):
        step(2 * t, 0)
        step(2 * t + 1, 1)
        return carry

    lax.fori_loop(0, qi // 2, pair, 0)
    odd = qi % 2 == 1

    @pl.when(odd)
    def _():
        step(qi - 1, 0)
        last_step(1)

    @pl.when(jnp.logical_not(odd))
    def _():
        last_step(0)

    for g in range(heads):
        o_ref[:, g * HEAD_DIM:(g + 1) * HEAD_DIM] = (acc_ref[g] / l_ref[g]).T.astype(o_ref.dtype)


def _forgetting_attention(q_t, qc, k, kc, v_t, *, tq, heads=2):
    h, _, s = q_t.shape
    nk = s // tq
    q_spec = pl.BlockSpec((heads, HEAD_DIM, tq), lambda hg, qi: (hg, 0, qi))
    k_spec = pl.BlockSpec((heads, nk, tq, HEAD_DIM), lambda hg, qi: (hg, 0, 0, 0))
    return pl.pallas_call(
        functools.partial(_attn_kernel, tq=tq, heads=heads),
        grid=(h // heads, nk),
        in_specs=[q_spec, q_spec, k_spec, k_spec,
                  pl.BlockSpec((heads, nk, HEAD_DIM, tq), lambda hg, qi: (hg, 0, 0, 0))],
        out_specs=pl.BlockSpec((tq, heads * HEAD_DIM), lambda hg, qi: (qi, hg)),
        out_shape=jax.ShapeDtypeStruct((s, h * HEAD_DIM), BF16),
        scratch_shapes=[pltpu.VMEM((heads, MXU_DEPTH, tq), BF16),
                        pltpu.VMEM((heads, nk, tq, MXU_DEPTH), BF16),
                        pltpu.VMEM((heads, 1, tq), F32), pltpu.VMEM((heads, 1, tq), F32),
                        pltpu.VMEM((heads, HEAD_DIM, tq), F32),
                        pltpu.VMEM((heads, tq, tq), F32), pltpu.VMEM((heads, tq, tq), F32),
                        pltpu.VMEM((heads, tq, tq), BF16), pltpu.VMEM((heads, tq, tq), BF16),
                        pltpu.VMEM((heads, 1, tq), F32), pltpu.VMEM((heads, 1, tq), F32),
                        pltpu.VMEM((heads, 1, tq), F32), pltpu.VMEM((heads, 1, tq), F32)],
        compiler_params=_params("parallel", "arbitrary"),
        name="forgetting_attention",
    )(q_t, qc, k, kc, v_t)


def _merge_kernel(x_ref, ac_ref, at_ref, gc_ref, ga_ref, wc_ref, wa_ref, wo_ref, ng_ref, o_ref, h_ref):
    yc = jnp.dot(ac_ref[...], wc_ref[...], preferred_element_type=F32)
    ya = jnp.dot(at_ref[...], wa_ref[...], preferred_element_type=F32)
    merged = gc_ref[...].astype(F32) * yc + ga_ref[...].astype(F32) * ya
    y = x_ref[...] + jnp.dot(merged.astype(BF16), wo_ref[...], preferred_element_type=F32)
    o_ref[...] = y
    h_ref[...] = _rms(y, ng_ref[...]).astype(BF16)


def _merge(x, a_conv, a_attn, gates, w_conv_out, w_attn_out, w_out, ffn_g, *, layer, tm=512):
    s, d = x.shape
    const = lambda i: (layer, 0, 0)
    once = pl.Buffered(1)
    row_spec = pl.BlockSpec((tm, d), lambda i: (i, 0))
    return pl.pallas_call(
        _merge_kernel,
        grid=(s // tm,),
        in_specs=[row_spec,
                  pl.BlockSpec((tm, CONV_CH), lambda i: (i, 0)),
                  pl.BlockSpec((tm, ATTN_W), lambda i: (i, 0)),
                  pl.BlockSpec((tm, d), lambda i: (i, 0)),
                  pl.BlockSpec((tm, d), lambda i: (i, 1)),
                  pl.BlockSpec((None, CONV_CH, d), const, pipeline_mode=once),
                  pl.BlockSpec((None, ATTN_W, d), const, pipeline_mode=once),
                  pl.BlockSpec((None, d, d), const, pipeline_mode=once),
                  pl.BlockSpec((1, d), lambda i: (0, 0))],
        out_specs=[row_spec, row_spec],
        out_shape=[jax.ShapeDtypeStruct((s, d), F32), jax.ShapeDtypeStruct((s, d), BF16)],
        compiler_params=_params("parallel"),
        name="gated_merge",
    )(x, a_conv, a_attn, gates, gates, w_conv_out, w_attn_out, w_out, ffn_g)


def _ffn_kernel(h_ref, wg_ref, wu_ref, wd_ref, o_ref, *, sub):
    @pl.when(pl.program_id(1) == 0)
    def _():
        o_ref[...] = jnp.zeros_like(o_ref)

    h = h_ref[...]
    for c in range(wd_ref.shape[0] // sub):
        cols = slice(c * sub, (c + 1) * sub)
        gate = jnp.dot(h, wg_ref[:, cols].astype(BF16), preferred_element_type=F32)
        up = jnp.dot(h, wu_ref[:, cols].astype(BF16), preferred_element_type=F32)
        a = (gate * jax.nn.sigmoid(gate) * up).astype(BF16)
        o_ref[...] += jnp.dot(a, wd_ref[cols, :].astype(BF16), preferred_element_type=F32)


def _ffn(h, w_gate_up, w_down, *, layer, tm=1024, th=512, sub=256):
    s, d = h.shape
    nh = FFN_HIDDEN // th
    return pl.pallas_call(
        functools.partial(_ffn_kernel, sub=sub),
        grid=(s // tm, nh),
        in_specs=[pl.BlockSpec((tm, d), lambda i, c: (i, 0)),
                  pl.BlockSpec((None, d, th), lambda i, c: (layer, 0, c)),
                  pl.BlockSpec((None, d, th), lambda i, c: (layer, 0, c + nh)),
                  pl.BlockSpec((None, th, d), lambda i, c: (layer, c, 0))],
        out_specs=pl.BlockSpec((tm, d), lambda i, c: (i, 0)),
        out_shape=jax.ShapeDtypeStruct((s, d), F32),
        compiler_params=_params("parallel", "arbitrary"),
        name="swiglu_ffn",
    )(h, w_gate_up, w_gate_up, w_down)


def _ple_kernel(x_ref, f_ref, g_ref, p_ref, wg_ref, wp_ref, ng_ref, *o_refs, final, sub):
    for r in range(x_ref.shape[0] // sub):
        rows = slice(r * sub, (r + 1) * sub)
        x = x_ref[rows, :] + f_ref[rows, :]
        h = _rms(x, g_ref[...]).astype(BF16)
        gate = jax.nn.sigmoid(jnp.dot(h, wg_ref[...], preferred_element_type=F32))
        emb = jnp.dot(p_ref[rows, :].astype(BF16), wp_ref[...], preferred_element_type=F32)
        y = x + gate * emb
        if final:
            o_refs[0][rows, :] = _rms(y, ng_ref[...])
        else:
            o_refs[0][rows, :] = y
            o_refs[1][rows, :] = _rms(y, ng_ref[...]).astype(BF16)


def _ple(x, f, g, p, w_gate, w_proj, next_g, *, layer, final, tm=512, sub=256):
    s, d = x.shape
    pd = p.shape[-1]
    const = lambda i: (0, 0)
    stacked = lambda i: (layer, 0, 0)
    once = pl.Buffered(1)
    row_spec = pl.BlockSpec((tm, d), lambda i: (i, 0))
    out_specs = [row_spec] if final else [row_spec, row_spec]
    out_shape = [jax.ShapeDtypeStruct((s, d), F32)] + ([] if final else [jax.ShapeDtypeStruct((s, d), BF16)])
    return pl.pallas_call(
        functools.partial(_ple_kernel, final=final, sub=sub),
        grid=(s // tm,),
        in_specs=[row_spec,
                  row_spec,
                  pl.BlockSpec((1, d), const),
                  pl.BlockSpec((None, tm, pd), lambda i: (layer, i, 0)),
                  pl.BlockSpec((None, d, d), stacked, pipeline_mode=once),
                  pl.BlockSpec((None, pd, d), stacked, pipeline_mode=once),
                  pl.BlockSpec((1, d), const)],
        out_specs=out_specs,
        out_shape=out_shape,
        compiler_params=_params("parallel"),
        name="ple_final" if final else "ple",
    )(x, f, g, p, w_gate, w_proj, next_g)


def kernel(x, p, norm_mix_g, w_in, b_forget, conv_w, conv_b, conv_ln_g, conv_ln_b, w_conv_out,
           w_attn_out, w_out, norm_ffn_g, w_gate_up, w_down, norm_ple_g, w_ple_gate, w_ple_proj, final_g):
    b, s, d = x.shape
    depth = w_in.shape[0]
    assert b == 1 and d == D_MODEL
    xs = x.reshape(s, d)
    row = lambda v: v.reshape(1, -1)
    o_f = 2 * CONV_CH + 3 * ATTN_W
    o_g = o_f + N_HEADS
    tq = 512
    w_conv_out, w_attn_out, w_out, w_ple_gate, w_ple_proj = (
        w.astype(BF16) for w in (w_conv_out, w_attn_out, w_out, w_ple_gate, w_ple_proj))
    p = p.reshape(depth, s, -1)
    w_t = jnp.swapaxes(w_in, 1, 2)
    r_glu, r_qkv = 0, 2 * CONV_CH
    h = _rmsnorm_cast(xs, row(norm_mix_g[0]))

    for i in range(depth):
        b_f = jnp.pad(b_forget[i], (0, LANES - N_HEADS)).reshape(1, LANES)

        u = _in_proj(h, w_t, [r_glu, r_glu + CONV_CH], [], _ep_glu, F32,
                     layer=i, n=CONV_CH, tm=2048, tn=512, name="in_proj_glu")
        q_t, k, v_t = _qkv_proj(h, w_t, r_qkv // ATTN_W, layer=i, tm=1024, tq=tq)
        gates = _in_proj(h, w_t, [o_g], [], _ep_sigmoid, BF16,
                         layer=i, n=2 * D_MODEL, tm=2048, tn=1024, name="in_proj_gates")
        log_f = _in_proj(h, w_t, [o_f], [b_f], _ep_log_forget, F32,
                         layer=i, n=LANES, tm=1024, tn=LANES, name="in_proj_forget")

        qc, kc = _cumsum_operands(log_f, tb=tq)
        a_attn = _forgetting_attention(q_t, qc, k, kc, v_t, tq=tq)

        a_conv = _conformer_conv(u, conv_w[i], row(conv_b[i]), row(conv_ln_g[i]), row(conv_ln_b[i]))

        xs, h_ffn = _merge(xs, a_conv, a_attn, gates, w_conv_out, w_attn_out, w_out, row(norm_ffn_g[i]), layer=i)
        f = _ffn(h_ffn, w_gate_up, w_down, layer=i)
        if i == depth - 1:
            (xs,) = _ple(xs, f, row(norm_ple_g[i]), p, w_ple_gate, w_ple_proj, row(final_g), layer=i, final=True)
        else:
            xs, h = _ple(xs, f, row(norm_ple_g[i]), p, w_ple_gate, w_ple_proj, row(norm_mix_g[i + 1]),
                         layer=i, final=False)
    return xs.reshape(b, s, d)
```
